```python
import math
import jax, jax.numpy as jnp
from jax import lax
import numpy as np

D_MODEL = 1024
BATCH = 16
SEQ = 4096
DEPTH = 1

MIX_WIDTH = D_MODEL
ATTN_WIDTH = MIX_WIDTH // 2
HEAD_DIM = 64
N_HEADS = ATTN_WIDTH // HEAD_DIM
SSM_WIDTH = MIX_WIDTH - ATTN_WIDTH
SSM_GROUP = 16
N_SSM_GROUPS = SSM_WIDTH // SSM_GROUP
STATE_DIM = 64
Q_BLOCK = 128
D_FF = ((8 * D_MODEL // 3 + 127) // 128) * 128
CONV_WIDTH = 3
IN_COLS = 3 * ATTN_WIDTH + N_HEADS + SSM_WIDTH
N_MOD = 6
EPS = 1e-6
NEG_INF = -1e30

kernel_name = "fox_s5_hymba_convffn_adaln"


def rmsnorm(x, g):
    x32 = x.astype(jnp.float32)
    y = x32 * lax.rsqrt(jnp.mean(x32 * x32, axis=-1, keepdims=True) + EPS)
    return (y * g.astype(jnp.float32)).astype(x.dtype)


def forgetting_attention(q, k, v, log_f):
    bsz, seq, nh, dh = q.shape
    nb = seq // Q_BLOCK
    cum = jnp.cumsum(log_f, axis=1).transpose(0, 2, 1)
    q_blocks = q.reshape(bsz, nb, Q_BLOCK, nh, dh).transpose(1, 0, 2, 3, 4)
    cum_blocks = cum.reshape(bsz, nh, nb, Q_BLOCK).transpose(2, 0, 1, 3)
    kpos = jnp.arange(seq)
    scale = dh ** -0.5

    def one_block(args):
        q_i, cum_i, i = args
        qpos = i * Q_BLOCK + jnp.arange(Q_BLOCK)
        s = jnp.einsum('bqhd,bkhd->bhqk', q_i, k,
                       preferred_element_type=jnp.float32) * scale
        s = s + cum_i[..., :, None] - cum[:, :, None, :]
        mask = kpos[None, :] <= qpos[:, None]
        s = jnp.where(mask, s, NEG_INF)
        p = jax.nn.softmax(s, axis=-1)
        return jnp.einsum('bhqk,bkhd->bqhd', p.astype(v.dtype), v)

    out = lax.map(one_block, (q_blocks, cum_blocks, jnp.arange(nb)))
    return out.transpose(1, 0, 2, 3, 4).reshape(bsz, seq, nh * dh)


def _scan_op(e1, e2):
    a1, b1 = e1
    a2, b2 = e2
    return a1 * a2, a2 * b1 + b2


def s5_ssm(u, a_re, a_im, log_dt, b_re, b_im, c_re, c_im, d_skip, w_glu, b_glu):
    bsz, seq, _ = u.shape
    f32 = jnp.float32
    ug = u.astype(f32).reshape(bsz, seq, N_SSM_GROUPS, SSM_GROUP)
    lam = lax.complex(a_re.astype(f32), a_im.astype(f32))
    dt = jnp.exp(log_dt.astype(f32))[:, None]
    lam_bar = jnp.exp(lam * dt)
    b_mat = lax.complex(b_re.astype(f32), b_im.astype(f32))
    b_bar = ((lam_bar - 1.0) / lam)[:, :, None] * b_mat
    bu = jnp.einsum('bsgc,gpc->bsgp', ug.astype(jnp.complex64), b_bar)
    a = jnp.broadcast_to(lam_bar, (1, seq) + lam_bar.shape)
    _, h = lax.associative_scan(_scan_op, (a, bu), axis=1)
    c_mat = lax.complex(c_re.astype(f32), c_im.astype(f32))
    y = jnp.real(jnp.einsum('bsgp,gcp->bsgc', h, c_mat)) + d_skip.astype(f32) * ug
    z = jax.nn.gelu(y)
    gate = jnp.einsum('bsgc,gcd->bsgd', z, w_glu.astype(f32)) + b_glu.astype(f32)
    out = z * jax.nn.sigmoid(gate)
    return out.reshape(bsz, seq, SSM_WIDTH).astype(u.dtype)


def causal_dwconv(h, w, b):
    ch = h.shape[-1]
    y = lax.conv_general_dilated(h, w.astype(h.dtype)[:, None, :], window_strides=(1,),
                                 padding=[(CONV_WIDTH - 1, 0)],
                                 dimension_numbers=('NWC', 'WIO', 'NWC'),
                                 feature_group_count=ch)
    return y + b.astype(h.dtype)


def setup_inputs(seed: int = 0) -> dict:
    key = jax.random.key(seed)
    ks = jax.random.split(key, 26)
    f32 = jnp.float32
    nrm = lambda k, shp, s: jax.random.normal(k, shp, f32) * s
    L, D, G, P, C = DEPTH, D_MODEL, N_SSM_GROUPS, STATE_DIM, SSM_GROUP
    a_im_base = math.pi * jnp.arange(P, dtype=f32)
    return {
        "x": nrm(ks[0], (BATCH, SEQ, D), 1.0),
        "c": nrm(ks[1], (BATCH, D), 1.0),
        "w_ada": nrm(ks[2], (L, D, N_MOD * D), 0.5 * D ** -0.5),
        "b_ada": nrm(ks[3], (L, N_MOD * D), 0.02),
        "g_mix": 1.0 + nrm(ks[4], (L, D), 0.02),
        "w_in": nrm(ks[5], (L, D, IN_COLS), D ** -0.5),
        "b_fgate": 3.0 + nrm(ks[6], (L, N_HEADS), 0.5),
        "a_re": -0.5 + nrm(ks[7], (L, G, P), 0.01),
        "a_im": a_im_base + nrm(ks[8], (L, G, P), 0.01),
        "log_dt": jax.random.uniform(ks[9], (L, G), f32, math.log(1e-3), math.log(1e-1)),
        "ssm_b_re": nrm(ks[10], (L, G, P, C), (2 * C) ** -0.5),
        "ssm_b_im": nrm(ks[11], (L, G, P, C), (2 * C) ** -0.5),
        "ssm_c_re": nrm(ks[12], (L, G, C, P), (2 * P) ** -0.5),
        "ssm_c_im": nrm(ks[13], (L, G, C, P), (2 * P) ** -0.5),
        "d_skip": nrm(ks[14], (L, G, C), 1.0),
        "w_glu": nrm(ks[15], (L, G, C, C), C ** -0.5),
        "b_glu": nrm(ks[16], (L, G, C), 0.02),
        "g_attn_out": 1.0 + nrm(ks[17], (L, ATTN_WIDTH), 0.02),
        "g_ssm_out": 1.0 + nrm(ks[18], (L, SSM_WIDTH), 0.02),
        "w_out": nrm(ks[19], (L, MIX_WIDTH, D), MIX_WIDTH ** -0.5),
        "g_ffn": 1.0 + nrm(ks[20], (L, D), 0.02),
        "w_up": nrm(ks[21], (L, D, 2 * D_FF), D ** -0.5),
        "conv_w": nrm(ks[22], (L, CONV_WIDTH, D_FF), CONV_WIDTH ** -0.5),
        "conv_b": nrm(ks[23], (L, D_FF), 0.02),
        "w_down": nrm(ks[24], (L, D_FF, D), D_FF ** -0.5),
        "g_final": 1.0 + nrm(ks[25], (D,), 0.02),
    }


def reference(x, c, w_ada, b_ada, g_mix, w_in, b_fgate, a_re, a_im, log_dt,
              ssm_b_re, ssm_b_im, ssm_c_re, ssm_c_im, d_skip, w_glu, b_glu,
              g_attn_out, g_ssm_out, w_out, g_ffn, w_up, conv_w, conv_b, w_down,
              g_final):
    bsz, seq, _ = x.shape
    silu_c = jax.nn.silu(c)
    for l in range(DEPTH):
        mod = (silu_c @ w_ada[l] + b_ada[l])[:, None, :]
        sh_m, sc_m, gt_m, sh_f, sc_f, gt_f = jnp.split(mod, N_MOD, axis=-1)

        h = rmsnorm(x, g_mix[l]) * (1.0 + sc_m) + sh_m
        proj = h @ w_in[l]
        q, k, v, f_logit, u = jnp.split(
            proj, [ATTN_WIDTH, 2 * ATTN_WIDTH, 3 * ATTN_WIDTH, 3 * ATTN_WIDTH + N_HEADS], axis=-1)
        q = q.reshape(bsz, seq, N_HEADS, HEAD_DIM)
        k = k.reshape(bsz, seq, N_HEADS, HEAD_DIM)
        v = v.reshape(bsz, seq, N_HEADS, HEAD_DIM)
        log_f = jax.nn.log_sigmoid(f_logit.astype(jnp.float32) + b_fgate[l].astype(jnp.float32))
        attn = rmsnorm(forgetting_attention(q, k, v, log_f), g_attn_out[l])
        ssm = rmsnorm(s5_ssm(u, a_re[l], a_im[l], log_dt[l], ssm_b_re[l], ssm_b_im[l],
                             ssm_c_re[l], ssm_c_im[l], d_skip[l], w_glu[l], b_glu[l]),
                      g_ssm_out[l])
        mix = jnp.concatenate([attn, ssm], axis=-1) @ w_out[l]
        x = x + gt_m * mix

        h = rmsnorm(x, g_ffn[l]) * (1.0 + sc_f) + sh_f
        gate_pre, val = jnp.split(h @ w_up[l], 2, axis=-1)
        gate_pre = causal_dwconv(gate_pre, conv_w[l], conv_b[l])
        y = (jax.nn.silu(gate_pre) * val) @ w_down[l]
        x = x + gt_f * y
    return rmsnorm(x, g_final)
```

```python
import functools
import math

import jax
import jax.numpy as jnp
from jax import lax
from jax.experimental import pallas as pl
from jax.experimental.pallas import tpu as pltpu

EPS = 1e-6
NEG_INF = -1e30
HEAD_DIM = 64
N_HEADS = 8
SSM_GROUP = 16
STATE_DIM = 64
CONV_WIDTH = 3
N_MOD = 6

LANES = 128
GROUPS_PER_CHUNK = 8
VMEM_LIMIT = 56 * 1024 * 1024
TM_IN = 512
TQ = 512
TM_FFN = 256
TT = 32

BF16 = jnp.bfloat16
F32 = jnp.float32


def _split3(a):
    hi = a.astype(BF16)
    r1 = a - hi.astype(F32)
    mid = r1.astype(BF16)
    lo = (r1 - mid.astype(F32)).astype(BF16)
    return hi, mid, lo


def _rms_scale(x):
    return lax.rsqrt(jnp.mean(x * x, axis=-1, keepdims=True) + EPS)


def _mod_kernel(c_ref, w_ref, b_ref, o_ref):
    c = c_ref[...]
    s = c * jax.nn.sigmoid(c)
    w = w_ref[...]
    s_hi = s.astype(BF16)
    s_lo = (s - s_hi.astype(F32)).astype(BF16)
    w_hi = w.astype(BF16)
    w_lo = (w - w_hi.astype(F32)).astype(BF16)
    acc = jnp.dot(s_hi, w_hi, preferred_element_type=F32)
    acc += jnp.dot(s_lo, w_hi, preferred_element_type=F32)
    acc += jnp.dot(s_hi, w_lo, preferred_element_type=F32)
    o_ref[...] = acc + b_ref[...]


def _modulation(c, w_ada, b_ada):
    bsz, d = c.shape
    n = w_ada.shape[1]
    tn = 512
    return pl.pallas_call(
        _mod_kernel,
        out_shape=jax.ShapeDtypeStruct((bsz, n), F32),
        grid=(n // tn,),
        in_specs=[pl.BlockSpec((bsz, d), lambda j: (0, 0)),
                  pl.BlockSpec((d, tn), lambda j: (0, j)),
                  pl.BlockSpec((1, tn), lambda j: (0, j))],
        out_specs=pl.BlockSpec((bsz, tn), lambda j: (0, j)),
        compiler_params=pltpu.CompilerParams(dimension_semantics=("arbitrary",)),
        name="modulation",
    )(c, w_ada, b_ada.reshape(1, n))


def _inproj_kernel(x_ref, mod_ref, g_ref, w_ref, wft_ref, bf_ref, tri_ref,
                   q_ref, k_ref, v_ref, f_ref, u_ref, carry_ref, *, d, aw):
    si = pl.program_id(1)
    tm = x_ref.shape[1]

    @pl.when(si == 0)
    def _():
        carry_ref[...] = jnp.zeros_like(carry_ref)

    x = x_ref[0]
    shift = mod_ref[0, :, 0:d]
    scale = mod_ref[0, :, d:2 * d]
    h = x * _rms_scale(x) * g_ref[...]
    h = h * (1.0 + scale) + shift
    hb = h.astype(BF16)

    proj = jnp.dot(hb, w_ref[...], preferred_element_type=F32)
    q_ref[0] = proj[:, 0:aw].astype(BF16)
    k_ref[0] = proj[:, aw:2 * aw].astype(BF16)
    v_ref[0] = proj[:, 2 * aw:3 * aw].astype(BF16)
    u_ref[...] = proj[:, 3 * aw:].astype(BF16)

    z = lax.dot_general(wft_ref[...], hb, (((1,), (1,)), ((), ())),
                        preferred_element_type=F32) + bf_ref[...]
    logf = jnp.minimum(z, 0.0) - jnp.log1p(jnp.exp(-jnp.abs(z)))
    pieces = jnp.concatenate(_split3(logf), axis=0)
    cs = jnp.dot(pieces, tri_ref[...], preferred_element_type=F32)
    nh = logf.shape[0]
    cum = cs[0:nh] + cs[nh:2 * nh] + cs[2 * nh:3 * nh]
    cum = cum + carry_ref[:, LANES - 1:LANES]
    f_ref[0] = cum
    carry_ref[...] = cum[:, tm - LANES:tm]


def _in_proj(x, mod3, g_mix, w_cat, wf_t, b_f, tm):
    bsz, seq, d = x.shape
    ncat = w_cat.shape[1]
    aw = N_HEADS * HEAD_DIM
    sw = ncat - 3 * aw
    nh = wf_t.shape[0]
    tri = (jnp.arange(tm)[:, None] <= jnp.arange(tm)[None, :]).astype(BF16)
    const = lambda b, s: (0, 0)
    kern = functools.partial(_inproj_kernel, d=d, aw=aw)
    return pl.pallas_call(
        kern,
        out_shape=(jax.ShapeDtypeStruct((bsz, seq, aw), BF16),
                   jax.ShapeDtypeStruct((bsz, seq, aw), BF16),
                   jax.ShapeDtypeStruct((bsz, seq, aw), BF16),
                   jax.ShapeDtypeStruct((bsz, nh, seq), F32),
                   jax.ShapeDtypeStruct((seq, bsz * sw), BF16)),
        grid=(bsz, seq // tm),
        in_specs=[pl.BlockSpec((1, tm, d), lambda b, s: (b, s, 0)),
                  pl.BlockSpec((1, 1, mod3.shape[2]), lambda b, s: (b, 0, 0)),
                  pl.BlockSpec((1, d), const),
                  pl.BlockSpec((d, ncat), const, pipeline_mode=pl.Buffered(1)),
                  pl.BlockSpec((nh, d), const),
                  pl.BlockSpec((nh, 1), const),
                  pl.BlockSpec((tm, tm), const, pipeline_mode=pl.Buffered(1))],
        out_specs=(pl.BlockSpec((1, tm, aw), lambda b, s: (b, s, 0)),
                   pl.BlockSpec((1, tm, aw), lambda b, s: (b, s, 0)),
                   pl.BlockSpec((1, tm, aw), lambda b, s: (b, s, 0)),
                   pl.BlockSpec((1, nh, tm), lambda b, s: (b, 0, s)),
                   pl.BlockSpec((tm, sw), lambda b, s: (s, b))),
        scratch_shapes=[pltpu.VMEM((nh, LANES), F32)],
        compiler_params=pltpu.CompilerParams(
            dimension_semantics=("arbitrary", "arbitrary"), vmem_limit_bytes=VMEM_LIMIT),
        name="in_proj",
    )(x, mod3, g_mix, w_cat, wf_t, b_f, tri)


def _attn_kernel(q_ref, k_ref, v_ref, f_ref, o_ref, *, tq):
    qi = pl.program_id(2)
    q0 = pl.multiple_of(qi * tq, tq)
    qpair = q_ref[0]
    lane = lax.broadcasted_iota(jnp.int32, (tq, LANES), 1)
    row = lax.broadcasted_iota(jnp.int32, (tq, tq), 0)
    col = lax.broadcasted_iota(jnp.int32, (tq, tq), 1)
    causal = col <= row
    zero = jnp.zeros_like(qpair)
    outs = []
    for hh in range(2):
        in_head = (lane >= hh * HEAD_DIM) & (lane < (hh + 1) * HEAD_DIM)
        qh = jnp.where(in_head, qpair, zero)
        fref = f_ref[0, 0, hh:hh + 1, pl.ds(q0, LANES)][:, 0:1]

        def scores(j0, hh=hh, qh=qh, fref=fref):
            kb = k_ref[0, pl.ds(j0, tq), :]
            s = lax.dot_general(qh, kb, (((1,), (1,)), ((), ())), preferred_element_type=F32)
            g = f_ref[0, 0, hh:hh + 1, pl.ds(j0, tq)] - fref
            return s - g

        def update(carry, s, j0):
            m, l, acc = carry
            m_new = jnp.maximum(m, jnp.max(s, axis=-1, keepdims=True))
            alpha = jnp.exp(m - m_new)
            p = jnp.exp(s - m_new)
            l = alpha * l + jnp.sum(p, axis=-1, keepdims=True)
            vb = v_ref[0, pl.ds(j0, tq), :]
            acc = alpha * acc + jnp.dot(p.astype(BF16), vb, preferred_element_type=F32)
            return m_new, l, acc

        def body(j, carry):
            j0 = pl.multiple_of(j * tq, tq)
            return update(carry, scores(j0), j0)

        init = (jnp.full((tq, 1), NEG_INF, F32), jnp.zeros((tq, 1), F32), jnp.zeros((tq, LANES), F32))
        carry = lax.fori_loop(0, qi, body, init)
        s_diag = jnp.where(causal, scores(q0), NEG_INF)
        _, l, acc = update(carry, s_diag, q0)
        outs.append(acc / l)
    o_ref[0] = jnp.where(lane < HEAD_DIM, outs[0], outs[1]).astype(BF16)


def _attention(q, k, v, f, tq):
    bsz, seq, aw = q.shape
    npair = aw // LANES
    f = f.reshape(bsz, npair, f.shape[1] // npair, seq)
    kern = functools.partial(_attn_kernel, tq=tq)
    return pl.pallas_call(
        kern,
        out_shape=jax.ShapeDtypeStruct((bsz, seq, aw), BF16),
        grid=(bsz, npair, seq // tq),
        in_specs=[pl.BlockSpec((1, tq, LANES), lambda b, h, i: (b, i, h)),
                  pl.BlockSpec((1, seq, LANES), lambda b, h, i: (b, 0, h)),
                  pl.BlockSpec((1, seq, LANES), lambda b, h, i: (b, 0, h)),
                  pl.BlockSpec((1, 1, f.shape[2], seq), lambda b, h, i: (b, h, 0, 0))],
        out_specs=pl.BlockSpec((1, tq, LANES), lambda b, h, i: (b, i, h)),
        compiler_params=pltpu.CompilerParams(
            dimension_semantics=("arbitrary", "arbitrary", "arbitrary"), vmem_limit_bytes=VMEM_LIMIT),
        name="fox_attention",
    )(q, k, v, f)


def _ssm_param_kernel(are_ref, aim_ref, ldt_ref, bre_ref, bim_ref,
                      lre_ref, lim_ref, bbre_ref, bbim_ref):
    a_re = are_ref[...]
    a_im = aim_ref[...]
    dt = jnp.exp(ldt_ref[...])
    mag = jnp.exp(a_re * dt)
    ang = a_im * dt
    l_re = mag * jnp.cos(ang)
    l_im = mag * jnp.sin(ang)
    lre_ref[...] = l_re
    lim_ref[...] = l_im
    n_re = l_re - 1.0
    den = a_re * a_re + a_im * a_im
    c_re = (n_re * a_re + l_im * a_im) / den
    c_im = (l_im * a_re - n_re * a_im) / den
    c_re = c_re[:, None, :]
    c_im = c_im[:, None, :]
    b_re = bre_ref[...]
    b_im = bim_ref[...]
    bbre_ref[...] = c_re * b_re - c_im * b_im
    bbim_ref[...] = c_re * b_im + c_im * b_re


def _ssm_params(a_re, a_im, log_dt, b_re_t, b_im_t):
    g, p = a_re.shape
    c = b_re_t.shape[1]
    return pl.pallas_call(
        _ssm_param_kernel,
        out_shape=(jax.ShapeDtypeStruct((g, p), F32), jax.ShapeDtypeStruct((g, p), F32),
                   jax.ShapeDtypeStruct((g, c, p), F32), jax.ShapeDtypeStruct((g, c, p), F32)),
        name="ssm_params",
    )(a_re, a_im, log_dt.reshape(g, 1), b_re_t, b_im_t)


def _ssm_kernel(u_ref, lre_ref, lim_ref, bmat_ref, cre_ref, cim_ref, dskip_ref, wglu_ref, bglu_ref,
                gout_ref, o_ref, bu_ref, h_ref, sre_ref, sim_ref, *, nb, tt):
    nchunk = bmat_ref.shape[0]
    cw = bmat_ref.shape[2] // 2
    iw = bmat_ref.shape[1]

    @pl.when(pl.program_id(0) == 0)
    def _():
        sre_ref[...] = jnp.zeros_like(sre_ref)
        sim_ref[...] = jnp.zeros_like(sim_ref)

    u = u_ref[...]
    ys = []
    for c in range(nchunk):
        base = 2 * cw * c
        bu_ref[:, base:base + 2 * cw] = jnp.dot(u[:, iw * c:iw * (c + 1)], bmat_ref[c],
                                                preferred_element_type=F32)
        a_re = jnp.broadcast_to(lre_ref[c:c + 1, :], (nb, cw))
        a_im = jnp.broadcast_to(lim_ref[c:c + 1, :], (nb, cw))

        def body(t, carry, base=base, a_re=a_re, a_im=a_im):
            h_re, h_im = carry
            r0 = pl.multiple_of(t * nb, nb)
            b_re = bu_ref[pl.ds(r0, nb), base:base + cw]
            b_im = bu_ref[pl.ds(r0, nb), base + cw:base + 2 * cw]
            n_re = a_re * h_re - a_im * h_im + b_re
            n_im = a_re * h_im + a_im * h_re + b_im
            h_ref[pl.ds(r0, nb), base:base + cw] = n_re.astype(BF16)
            h_ref[pl.ds(r0, nb), base + cw:base + 2 * cw] = n_im.astype(BF16)
            return n_re, n_im

        init = (sre_ref[:, cw * c:cw * (c + 1)], sim_ref[:, cw * c:cw * (c + 1)])
        h_re, h_im = lax.fori_loop(0, tt, body, init, unroll=4)
        sre_ref[:, cw * c:cw * (c + 1)] = h_re
        sim_ref[:, cw * c:cw * (c + 1)] = h_im
        y = jnp.dot(h_ref[:, base:base + cw], cre_ref[c], preferred_element_type=F32)
        y -= jnp.dot(h_ref[:, base + cw:base + 2 * cw], cim_ref[c], preferred_element_type=F32)
        ys.append(y)
    y = jnp.concatenate(ys, axis=-1) + dskip_ref[...] * u.astype(F32)
    z = jax.nn.gelu(y, approximate=True)
    gate = jnp.dot(z.astype(BF16), wglu_ref[...], preferred_element_type=F32) + bglu_ref[...]
    out = z * jax.nn.sigmoid(gate)
    o_ref[...] = (out * _rms_scale(out) * gout_ref[...]).astype(BF16)


def _ssm(u_tb, l_re, l_im, bmat, cre, cim, dskip, wglu_bd, bglu, g_out, nb, tt):
    rows_total, w = u_tb.shape
    rows = nb * tt
    nchunk, iw, cw2 = bmat.shape
    cw = cw2 // 2
    const2 = lambda i: (0, 0)
    const3 = lambda i: (0, 0, 0)
    kern = functools.partial(_ssm_kernel, nb=nb, tt=tt)
    return pl.pallas_call(
        kern,
        out_shape=jax.ShapeDtypeStruct((rows_total, w), BF16),
        grid=(rows_total // rows,),
        in_specs=[pl.BlockSpec((rows, w), lambda i: (i, 0)),
                  pl.BlockSpec((nchunk, cw), const2),
                  pl.BlockSpec((nchunk, cw), const2),
                  pl.BlockSpec((nchunk, iw, cw2), const3),
                  pl.BlockSpec((nchunk, cw, iw), const3),
                  pl.BlockSpec((nchunk, cw, iw), const3),
                  pl.BlockSpec((1, w), const2),
                  pl.BlockSpec((w, w), const2),
                  pl.BlockSpec((1, w), const2),
                  pl.BlockSpec((1, w), const2)],
        out_specs=pl.BlockSpec((rows, w), lambda i: (i, 0)),
        scratch_shapes=[pltpu.VMEM((rows, nchunk * cw2), F32),
                        pltpu.VMEM((rows, nchunk * cw2), BF16),
                        pltpu.VMEM((nb, nchunk * cw), F32),
                        pltpu.VMEM((nb, nchunk * cw), F32)],
        compiler_params=pltpu.CompilerParams(
            dimension_semantics=("arbitrary",), vmem_limit_bytes=VMEM_LIMIT),
        name="s5_ssm",
    )(u_tb, l_re, l_im, bmat, cre, cim, dskip, wglu_bd, bglu, g_out)


def _outffn_kernel(x_ref, attn_ref, ssm_ref, mod_ref, gattn_ref, wout_ref, gffn_ref, wup_ref,
                   convw_ref, convb_ref, wdown_ref, gfin_ref, o_ref, carry_ref, *, d, dff):
    si = pl.program_id(1)
    tm = x_ref.shape[1]

    @pl.when(si == 0)
    def _():
        carry_ref[...] = jnp.zeros_like(carry_ref)

    gt_m = mod_ref[0, :, 2 * d:3 * d]
    sh_f = mod_ref[0, :, 3 * d:4 * d]
    sc_f = mod_ref[0, :, 4 * d:5 * d]
    gt_f = mod_ref[0, :, 5 * d:6 * d]

    a = attn_ref[0].astype(F32)
    an = (a * _rms_scale(a) * gattn_ref[...]).astype(BF16)
    mixin = jnp.concatenate([an, ssm_ref[...]], axis=-1)
    mix = jnp.dot(mixin, wout_ref[...], preferred_element_type=F32)
    x1 = x_ref[0] + gt_m * mix

    h = x1 * _rms_scale(x1) * gffn_ref[...]
    h = (h * (1.0 + sc_f) + sh_f).astype(BF16)
    up = jnp.dot(h, wup_ref[...], preferred_element_type=F32)
    gp = up[:, 0:dff]
    val = up[:, dff:2 * dff]

    prev = carry_ref[...]
    r8 = lax.broadcasted_iota(jnp.int32, prev.shape, 0)

    def shifted(k):
        body = pltpu.roll(gp, k, 0)
        head = jnp.where(r8 < k, pltpu.roll(prev, k, 0), body[0:8])
        return jnp.concatenate([head, body[8:]], axis=0)

    conv = (convw_ref[0:1, :] * shifted(2) + convw_ref[1:2, :] * shifted(1)
            + convw_ref[2:3, :] * gp + convb_ref[...])
    carry_ref[...] = gp[tm - 8:tm]
    act = (conv * jax.nn.sigmoid(conv) * val).astype(BF16)
    y = jnp.dot(act, wdown_ref[...], preferred_element_type=F32)
    x2 = x1 + gt_f * y
    o_ref[0] = x2 * _rms_scale(x2) * gfin_ref[...]


def _out_ffn(x, attn, ssm_sb, mod3, g_attn, w_out, g_ffn, w_up, conv_w, conv_b, w_down, g_final, tm):
    bsz, seq, d = x.shape
    aw = attn.shape[2]
    sw = ssm_sb.shape[1] // bsz
    dff = w_down.shape[0]
    const = lambda b, s: (0, 0)
    single = pl.Buffered(1)
    kern = functools.partial(_outffn_kernel, d=d, dff=dff)
    return pl.pallas_call(
        kern,
        out_shape=jax.ShapeDtypeStruct((bsz, seq, d), F32),
        grid=(bsz, seq // tm),
        in_specs=[pl.BlockSpec((1, tm, d), lambda b, s: (b, s, 0)),
                  pl.BlockSpec((1, tm, aw), lambda b, s: (b, s, 0)),
                  pl.BlockSpec((tm, sw), lambda b, s: (s, b)),
                  pl.BlockSpec((1, 1, mod3.shape[2]), lambda b, s: (b, 0, 0)),
                  pl.BlockSpec((1, aw), const),
                  pl.BlockSpec((aw + sw, d), const, pipeline_mode=single),
                  pl.BlockSpec((1, d), const),
                  pl.BlockSpec((d, 2 * dff), const, pipeline_mode=single),
                  pl.BlockSpec((CONV_WIDTH, dff), const),
                  pl.BlockSpec((1, dff), const),
                  pl.BlockSpec((dff, d), const, pipeline_mode=single),
                  pl.BlockSpec((1, d), const)],
        out_specs=pl.BlockSpec((1, tm, d), lambda b, s: (b, s, 0)),
        scratch_shapes=[pltpu.VMEM((8, dff), F32)],
        compiler_params=pltpu.CompilerParams(
            dimension_semantics=("arbitrary", "arbitrary"), vmem_limit_bytes=VMEM_LIMIT),
        name="out_ffn",
    )(x, attn, ssm_sb, mod3, g_attn, w_out, g_ffn, w_up, conv_w, conv_b, w_down, g_final)


def _block_diag(blocks):
    n, r, c = blocks.shape
    eye = jnp.eye(n, dtype=blocks.dtype)
    return (blocks[:, :, None, :] * eye[:, None, :, None]).reshape(n * r, n * c)


def _chunked_block_diag(blocks, per):
    g, r, c = blocks.shape
    return jax.vmap(_block_diag)(blocks.reshape(g // per, per, r, c))


def kernel(x, c, w_ada, b_ada, g_mix, w_in, b_fgate, a_re, a_im, log_dt, ssm_b_re, ssm_b_im,
           ssm_c_re, ssm_c_im, d_skip, w_glu, b_glu, g_attn_out, g_ssm_out, w_out, g_ffn, w_up,
           conv_w, conv_b, w_down, g_final):
    bsz, seq, d = x.shape
    depth = w_ada.shape[0]
    aw = N_HEADS * HEAD_DIM
    ngroups = a_re.shape[1]
    sw = ngroups * SSM_GROUP
    per = GROUPS_PER_CHUNK
    tm_in = min(TM_IN, seq)
    tq = min(TQ, seq)
    tm_ffn = min(TM_FFN, seq)
    tt = min(TT, seq)
    assert bsz % 8 == 0 and seq % tm_in == 0 and seq % tq == 0 and seq % tt == 0 and ngroups % per == 0

    for l in range(depth):
        mod3 = _modulation(c, w_ada[l], b_ada[l]).reshape(bsz, 1, N_MOD * d)

        w = w_in[l]
        w_cat = jnp.concatenate([w[:, 0:aw] * (HEAD_DIM ** -0.5), w[:, aw:3 * aw],
                                 w[:, 3 * aw + N_HEADS:]], axis=1).astype(BF16)
        wf_t = w[:, 3 * aw:3 * aw + N_HEADS].T.astype(BF16)
        q, k, v, f, u_sb = _in_proj(x, mod3, g_mix[l].reshape(1, d), w_cat, wf_t,
                                    b_fgate[l].reshape(N_HEADS, 1), tm_in)

        attn = _attention(q, k, v, f, tq)

        l_re, l_im, bb_re, bb_im = _ssm_params(
            a_re[l], a_im[l], log_dt[l],
            ssm_b_re[l].transpose(0, 2, 1), ssm_b_im[l].transpose(0, 2, 1))
        bmat = jnp.concatenate([_chunked_block_diag(bb_re, per), _chunked_block_diag(bb_im, per)],
                               axis=-1).astype(BF16)
        cre = _chunked_block_diag(ssm_c_re[l].transpose(0, 2, 1), per).astype(BF16)
        cim = _chunked_block_diag(ssm_c_im[l].transpose(0, 2, 1), per).astype(BF16)
        wglu_bd = _block_diag(w_glu[l]).astype(BF16)
        ssm_sb = _ssm(u_sb.reshape(seq * bsz, sw),
                      l_re.reshape(ngroups // per, per * STATE_DIM),
                      l_im.reshape(ngroups // per, per * STATE_DIM),
                      bmat, cre, cim, d_skip[l].reshape(1, sw), wglu_bd, b_glu[l].reshape(1, sw),
                      g_ssm_out[l].reshape(1, sw), bsz, tt).reshape(seq, bsz * sw)

        assert depth == 1
        x = _out_ffn(x, attn, ssm_sb, mod3, g_attn_out[l].reshape(1, aw), w_out[l].astype(BF16),
                     g_ffn[l].reshape(1, d), w_up[l].astype(BF16), conv_w[l], conv_b[l].reshape(1, -1),
                     w_down[l].astype(BF16), g_final.reshape(1, d), tm_ffn)
    return x
```

```python
import functools
import math

import jax
import jax.numpy as jnp
from jax import lax
from jax.experimental import pallas as pl
from jax.experimental.pallas import tpu as pltpu

EPS = 1e-6
NEG_INF = -1e30
HEAD_DIM = 64
N_HEADS = 8
SSM_GROUP = 16
STATE_DIM = 64
CONV_WIDTH = 3
N_MOD = 6

LANES = 128
GROUPS_PER_CHUNK = 8
VMEM_LIMIT = 56 * 1024 * 1024
TM_IN = 512
TQ = 512
ATTN_PREP = 512
LOG2E = math.log2(math.e)
TM_FFN = 256
TT = 32

BF16 = jnp.bfloat16
F32 = jnp.float32


def _split3(a):
    hi = a.astype(BF16)
    r1 = a - hi.astype(F32)
    mid = r1.astype(BF16)
    lo = (r1 - mid.astype(F32)).astype(BF16)
    return hi, mid, lo


def _rms_scale(x):
    return lax.rsqrt(jnp.mean(x * x, axis=-1, keepdims=True) + EPS)


def _mod_kernel(c_ref, w_ref, b_ref, o_ref):
    c = c_ref[...]
    s = c * jax.nn.sigmoid(c)
    w = w_ref[...]
    s_hi = s.astype(BF16)
    s_lo = (s - s_hi.astype(F32)).astype(BF16)
    w_hi = w.astype(BF16)
    w_lo = (w - w_hi.astype(F32)).astype(BF16)
    acc = jnp.dot(s_hi, w_hi, preferred_element_type=F32)
    acc += jnp.dot(s_lo, w_hi, preferred_element_type=F32)
    acc += jnp.dot(s_hi, w_lo, preferred_element_type=F32)
    o_ref[...] = acc + b_ref[...]


def _modulation(c, w_ada, b_ada):
    bsz, d = c.shape
    n = w_ada.shape[1]
    tn = 512
    return pl.pallas_call(
        _mod_kernel,
        out_shape=jax.ShapeDtypeStruct((bsz, n), F32),
        grid=(n // tn,),
        in_specs=[pl.BlockSpec((bsz, d), lambda j: (0, 0)),
                  pl.BlockSpec((d, tn), lambda j: (0, j)),
                  pl.BlockSpec((1, tn), lambda j: (0, j))],
        out_specs=pl.BlockSpec((bsz, tn), lambda j: (0, j)),
        compiler_params=pltpu.CompilerParams(dimension_semantics=("arbitrary",)),
        name="modulation",
    )(c, w_ada, b_ada.reshape(1, n))


def _inproj_kernel(x_ref, mod_ref, g_ref, w_ref, wft_ref, bf_ref, tri_ref,
                   q_ref, k_ref, v_ref, f_ref, u_ref, carry_ref, *, d, aw):
    si = pl.program_id(1)
    tm = x_ref.shape[1]

    @pl.when(si == 0)
    def _():
        carry_ref[...] = jnp.zeros_like(carry_ref)

    x = x_ref[0]
    shift = mod_ref[0, :, 0:d]
    scale = mod_ref[0, :, d:2 * d]
    h = x * _rms_scale(x) * g_ref[...]
    h = h * (1.0 + scale) + shift
    hb = h.astype(BF16)

    proj = jnp.dot(hb, w_ref[...], preferred_element_type=F32)
    q_ref[0] = (proj[:, 0:aw] * (HEAD_DIM ** -0.5 * LOG2E)).astype(BF16)
    k_ref[0] = proj[:, aw:2 * aw].astype(BF16)
    v_ref[0] = proj[:, 2 * aw:3 * aw].astype(BF16)
    u_ref[...] = proj[:, 3 * aw:].astype(BF16)

    z = lax.dot_general(wft_ref[...], hb, (((1,), (1,)), ((), ())),
                        preferred_element_type=F32) + bf_ref[...]
    logf = jnp.minimum(z, 0.0) - jnp.log1p(jnp.exp(-jnp.abs(z)))
    pieces = jnp.concatenate(_split3(logf), axis=0)
    cs = jnp.dot(pieces, tri_ref[...], preferred_element_type=F32)
    nh = logf.shape[0]
    cum = cs[0:nh] + cs[nh:2 * nh] + cs[2 * nh:3 * nh]
    cum = cum + carry_ref[:, LANES - 1:LANES]
    f_ref[0] = cum * LOG2E
    carry_ref[...] = cum[:, tm - LANES:tm]


def _in_proj(x, mod3, g_mix, w_cat, wf_t, b_f, tm):
    bsz, seq, d = x.shape
    ncat = w_cat.shape[1]
    aw = N_HEADS * HEAD_DIM
    sw = ncat - 3 * aw
    nh = wf_t.shape[0]
    tri = (jnp.arange(tm)[:, None] <= jnp.arange(tm)[None, :]).astype(BF16)
    const = lambda b, s: (0, 0)
    kern = functools.partial(_inproj_kernel, d=d, aw=aw)
    return pl.pallas_call(
        kern,
        out_shape=(jax.ShapeDtypeStruct((bsz, seq, aw), BF16),
                   jax.ShapeDtypeStruct((bsz, seq, aw), BF16),
                   jax.ShapeDtypeStruct((bsz, seq, aw), BF16),
                   jax.ShapeDtypeStruct((bsz, nh, seq), F32),
                   jax.ShapeDtypeStruct((seq, bsz * sw), BF16)),
        grid=(bsz, seq // tm),
        in_specs=[pl.BlockSpec((1, tm, d), lambda b, s: (b, s, 0)),
                  pl.BlockSpec((1, 1, mod3.shape[2]), lambda b, s: (b, 0, 0)),
                  pl.BlockSpec((1, d), const),
                  pl.BlockSpec((d, ncat), const, pipeline_mode=pl.Buffered(1)),
                  pl.BlockSpec((nh, d), const),
                  pl.BlockSpec((nh, 1), const),
                  pl.BlockSpec((tm, tm), const, pipeline_mode=pl.Buffered(1))],
        out_specs=(pl.BlockSpec((1, tm, aw), lambda b, s: (b, s, 0)),
                   pl.BlockSpec((1, tm, aw), lambda b, s: (b, s, 0)),
                   pl.BlockSpec((1, tm, aw), lambda b, s: (b, s, 0)),
                   pl.BlockSpec((1, nh, tm), lambda b, s: (b, 0, s)),
                   pl.BlockSpec((tm, sw), lambda b, s: (s, b))),
        scratch_shapes=[pltpu.VMEM((nh, LANES), F32)],
        compiler_params=pltpu.CompilerParams(
            dimension_semantics=("arbitrary", "arbitrary"), vmem_limit_bytes=VMEM_LIMIT),
        name="in_proj",
    )(x, mod3, g_mix, w_cat, wf_t, b_f, tri)


def _attn_kernel(q_ref, k_ref, v_ref, f_ref, o_ref, qx_ref, kx_ref, vx_ref, m_ref, acc_ref, s_ref,
                 *, tq, prep):
    qi = pl.program_id(2)
    seq = k_ref.shape[1]
    q0 = pl.multiple_of(qi * tq, tq)
    data_lo = (0, HEAD_DIM)
    extra_lo = (HEAD_DIM, 0)

    @pl.when(qi == 0)
    def _():
        def chunk(ci, _):
            r0 = pl.multiple_of(ci * prep, prep)
            lane = lax.broadcasted_iota(jnp.int32, (prep, LANES), 1)
            sub = lax.broadcasted_iota(jnp.int32, (LANES, prep), 0)
            qp = q_ref[0, pl.ds(r0, prep), :]
            kp = k_ref[0, pl.ds(r0, prep), :]
            vp = v_ref[0, pl.ds(r0, prep), :]
            for hh in range(2):
                e0 = extra_lo[hh]
                hi, mid, lo = [x.astype(F32) for x in _split3(f_ref[0, 0, hh:hh + 1, pl.ds(r0, prep)])]
                one = jnp.ones_like(hi)

                def columns(vals, e0=e0):
                    out = jnp.zeros((LANES, prep), F32)
                    for i, val in enumerate(vals):
                        out = jnp.where(sub == e0 + i, val, out)
                    return out.T.astype(BF16)

                q_extra = columns([hi, mid, lo, one, one, one])
                k_extra = columns([one, one, one, -hi, -mid, -lo])
                in_head = (lane >= data_lo[hh]) & (lane < data_lo[hh] + HEAD_DIM)
                qx_ref[hh, pl.ds(r0, prep), :] = jnp.where(in_head, qp, q_extra)
                kx_ref[hh, pl.ds(r0, prep), :] = jnp.where(in_head, kp, k_extra)
                ones_col = jnp.where(lane == e0, 1.0, 0.0).astype(BF16)
                vx_ref[hh, pl.ds(r0, prep), :] = jnp.where(in_head, vp, ones_col)
            return 0
        lax.fori_loop(0, seq // prep, chunk, 0)

    m_ref[...] = jnp.full(m_ref.shape, NEG_INF, F32)
    acc_ref[...] = jnp.zeros(acc_ref.shape, F32)

    def qk(slot, j0):
        for hh in range(2):
            qx = qx_ref[hh, pl.ds(q0, tq), :]
            kb = kx_ref[hh, pl.ds(j0, tq), :]
            s_ref[slot, hh] = lax.dot_general(qx, kb, (((1,), (1,)), ((), ())),
                                              preferred_element_type=F32)

    def softmax_pv(slot, j0, mask=None):
        for hh in range(2):
            s = s_ref[slot, hh]
            if mask is not None:
                s = jnp.where(mask, s, NEG_INF)
            m_old = m_ref[hh]
            m_new = jnp.maximum(m_old, jnp.max(s, axis=-1, keepdims=True))
            alpha = jnp.exp2(m_old - m_new)
            p = jnp.exp2(s - jnp.concatenate([m_new] * (tq // LANES), axis=1)).astype(BF16)
            vb = vx_ref[hh, pl.ds(j0, tq), :]
            acc_ref[hh] = alpha * acc_ref[hh] + jnp.dot(p, vb, preferred_element_type=F32)
            m_ref[hh] = m_new

    def blk(j):
        return pl.multiple_of(j * tq, tq)

    def step(j, slot):
        qk(1 - slot, blk(j + 1))
        softmax_pv(slot, blk(j))

    qk(0, 0)

    def body(t, _):
        step(2 * t, 0)
        step(2 * t + 1, 1)
        return 0

    lax.fori_loop(0, qi // 2, body, 0)
    row = lax.broadcasted_iota(jnp.int32, (tq, tq), 0)
    col = lax.broadcasted_iota(jnp.int32, (tq, tq), 1)
    causal = col <= row

    @pl.when(qi % 2 == 1)
    def _():
        step(qi - 1, 0)
        softmax_pv(1, q0, causal)

    @pl.when(qi % 2 == 0)
    def _():
        softmax_pv(0, q0, causal)

    lane = lax.broadcasted_iota(jnp.int32, (tq, LANES), 1)
    a0 = acc_ref[0]
    a1 = acc_ref[1]
    o0 = a0 / a0[:, extra_lo[0]:extra_lo[0] + 1]
    o1 = a1 / a1[:, extra_lo[1]:extra_lo[1] + 1]
    o_ref[0] = jnp.where(lane < HEAD_DIM, o0, o1).astype(BF16)


def _attention(q, k, v, f, tq):
    bsz, seq, aw = q.shape
    npair = aw // LANES
    f = f.reshape(bsz, npair, f.shape[1] // npair, seq)
    prep = min(ATTN_PREP, seq)
    kern = functools.partial(_attn_kernel, tq=tq, prep=prep)
    full = lambda b, h, i: (b, 0, h)
    return pl.pallas_call(
        kern,
        out_shape=jax.ShapeDtypeStruct((bsz, seq, aw), BF16),
        grid=(bsz, npair, seq // tq),
        in_specs=[pl.BlockSpec((1, seq, LANES), full),
                  pl.BlockSpec((1, seq, LANES), full),
                  pl.BlockSpec((1, seq, LANES), full),
                  pl.BlockSpec((1, 1, f.shape[2], seq), lambda b, h, i: (b, h, 0, 0))],
        out_specs=pl.BlockSpec((1, tq, LANES), lambda b, h, i: (b, i, h)),
        scratch_shapes=[pltpu.VMEM((2, seq, LANES), BF16),
                        pltpu.VMEM((2, seq, LANES), BF16),
                        pltpu.VMEM((2, seq, LANES), BF16),
                        pltpu.VMEM((2, tq, LANES), F32),
                        pltpu.VMEM((2, tq, LANES), F32),
                        pltpu.VMEM((2, 2, tq, tq), F32)],
        compiler_params=pltpu.CompilerParams(
            dimension_semantics=("arbitrary", "arbitrary", "arbitrary"), vmem_limit_bytes=VMEM_LIMIT),
        name="fox_attention",
    )(q, k, v, f)


def _ssm_param_kernel(are_ref, aim_ref, ldt_ref, bre_ref, bim_ref,
                      lre_ref, lim_ref, bbre_ref, bbim_ref):
    a_re = are_ref[...]
    a_im = aim_ref[...]
    dt = jnp.exp(ldt_ref[...])
    mag = jnp.exp(a_re * dt)
    ang = a_im * dt
    l_re = mag * jnp.cos(ang)
    l_im = mag * jnp.sin(ang)
    lre_ref[...] = l_re
    lim_ref[...] = l_im
    n_re = l_re - 1.0
    den = a_re * a_re + a_im * a_im
    c_re = (n_re * a_re + l_im * a_im) / den
    c_im = (l_im * a_re - n_re * a_im) / den
    c_re = c_re[:, None, :]
    c_im = c_im[:, None, :]
    b_re = bre_ref[...]
    b_im = bim_ref[...]
    bbre_ref[...] = c_re * b_re - c_im * b_im
    bbim_ref[...] = c_re * b_im + c_im * b_re


def _ssm_params(a_re, a_im, log_dt, b_re_t, b_im_t):
    g, p = a_re.shape
    c = b_re_t.shape[1]
    return pl.pallas_call(
        _ssm_param_kernel,
        out_shape=(jax.ShapeDtypeStruct((g, p), F32), jax.ShapeDtypeStruct((g, p), F32),
                   jax.ShapeDtypeStruct((g, c, p), F32), jax.ShapeDtypeStruct((g, c, p), F32)),
        name="ssm_params",
    )(a_re, a_im, log_dt.reshape(g, 1), b_re_t, b_im_t)


def _ssm_kernel(u_ref, lre_ref, lim_ref, bmat_ref, cre_ref, cim_ref, dskip_ref, wglu_ref, bglu_ref,
                gout_ref, o_ref, bu_ref, h_ref, sre_ref, sim_ref, *, nb, tt):
    nchunk = bmat_ref.shape[0]
    cw = bmat_ref.shape[2] // 2
    iw = bmat_ref.shape[1]

    @pl.when(pl.program_id(0) == 0)
    def _():
        sre_ref[...] = jnp.zeros_like(sre_ref)
        sim_ref[...] = jnp.zeros_like(sim_ref)

    u = u_ref[...]
    ys = []
    for c in range(nchunk):
        base = 2 * cw * c
        bu_ref[:, base:base + 2 * cw] = jnp.dot(u[:, iw * c:iw * (c + 1)], bmat_ref[c],
                                                preferred_element_type=F32)
        a_re = jnp.broadcast_to(lre_ref[c:c + 1, :], (nb, cw))
        a_im = jnp.broadcast_to(lim_ref[c:c + 1, :], (nb, cw))

        def body(t, carry, base=base, a_re=a_re, a_im=a_im):
            h_re, h_im = carry
            r0 = pl.multiple_of(t * nb, nb)
            b_re = bu_ref[pl.ds(r0, nb), base:base + cw]
            b_im = bu_ref[pl.ds(r0, nb), base + cw:base + 2 * cw]
            n_re = a_re * h_re - a_im * h_im + b_re
            n_im = a_re * h_im + a_im * h_re + b_im
            h_ref[pl.ds(r0, nb), base:base + cw] = n_re.astype(BF16)
            h_ref[pl.ds(r0, nb), base + cw:base + 2 * cw] = n_im.astype(BF16)
            return n_re, n_im

        init = (sre_ref[:, cw * c:cw * (c + 1)], sim_ref[:, cw * c:cw * (c + 1)])
        h_re, h_im = lax.fori_loop(0, tt, body, init, unroll=4)
        sre_ref[:, cw * c:cw * (c + 1)] = h_re
        sim_ref[:, cw * c:cw * (c + 1)] = h_im
        y = jnp.dot(h_ref[:, base:base + cw], cre_ref[c], preferred_element_type=F32)
        y -= jnp.dot(h_ref[:, base + cw:base + 2 * cw], cim_ref[c], preferred_element_type=F32)
        ys.append(y)
    y = jnp.concatenate(ys, axis=-1) + dskip_ref[...] * u.astype(F32)
    z = jax.nn.gelu(y, approximate=True)
    gate = jnp.dot(z.astype(BF16), wglu_ref[...], preferred_element_type=F32) + bglu_ref[...]
    out = z * jax.nn.sigmoid(gate)
    o_ref[...] = (out * _rms_scale(out) * gout_ref[...]).astype(BF16)


def _ssm(u_tb, l_re, l_im, bmat, cre, cim, dskip, wglu_bd, bglu, g_out, nb, tt):
    rows_total, w = u_tb.shape
    rows = nb * tt
    nchunk, iw, cw2 = bmat.shape
    cw = cw2 // 2
    const2 = lambda i: (0, 0)
    const3 = lambda i: (0, 0, 0)
    kern = functools.partial(_ssm_kernel, nb=nb, tt=tt)
    return pl.pallas_call(
        kern,
        out_shape=jax.ShapeDtypeStruct((rows_total, w), BF16),
        grid=(rows_total // rows,),
        in_specs=[pl.BlockSpec((rows, w), lambda i: (i, 0)),
                  pl.BlockSpec((nchunk, cw), const2),
                  pl.BlockSpec((nchunk, cw), const2),
                  pl.BlockSpec((nchunk, iw, cw2), const3),
                  pl.BlockSpec((nchunk, cw, iw), const3),
                  pl.BlockSpec((nchunk, cw, iw), const3),
                  pl.BlockSpec((1, w), const2),
                  pl.BlockSpec((w, w), const2),
                  pl.BlockSpec((1, w), const2),
                  pl.BlockSpec((1, w), const2)],
        out_specs=pl.BlockSpec((rows, w), lambda i: (i, 0)),
        scratch_shapes=[pltpu.VMEM((rows, nchunk * cw2), F32),
                        pltpu.VMEM((rows, nchunk * cw2), BF16),
                        pltpu.VMEM((nb, nchunk * cw), F32),
                        pltpu.VMEM((nb, nchunk * cw), F32)],
        compiler_params=pltpu.CompilerParams(
            dimension_semantics=("arbitrary",), vmem_limit_bytes=VMEM_LIMIT),
        name="s5_ssm",
    )(u_tb, l_re, l_im, bmat, cre, cim, dskip, wglu_bd, bglu, g_out)


def _outffn_kernel(x_ref, attn_ref, ssm_ref, mod_ref, gattn_ref, wout_ref, gffn_ref, wup_ref,
                   convw_ref, convb_ref, wdown_ref, gfin_ref, o_ref, carry_ref, *, d, dff):
    si = pl.program_id(1)
    tm = x_ref.shape[1]

    @pl.when(si == 0)
    def _():
        carry_ref[...] = jnp.zeros_like(carry_ref)

    gt_m = mod_ref[0, :, 2 * d:3 * d]
    sh_f = mod_ref[0, :, 3 * d:4 * d]
    sc_f = mod_ref[0, :, 4 * d:5 * d]
    gt_f = mod_ref[0, :, 5 * d:6 * d]

    a = attn_ref[0].astype(F32)
    an = (a * _rms_scale(a) * gattn_ref[...]).astype(BF16)
    mixin = jnp.concatenate([an, ssm_ref[...]], axis=-1)
    mix = jnp.dot(mixin, wout_ref[...], preferred_element_type=F32)
    x1 = x_ref[0] + gt_m * mix

    h = x1 * _rms_scale(x1) * gffn_ref[...]
    h = (h * (1.0 + sc_f) + sh_f).astype(BF16)
    up = jnp.dot(h, wup_ref[...], preferred_element_type=F32)
    gp = up[:, 0:dff]
    val = up[:, dff:2 * dff]

    prev = carry_ref[...]
    r8 = lax.broadcasted_iota(jnp.int32, prev.shape, 0)

    def shifted(k):
        body = pltpu.roll(gp, k, 0)
        head = jnp.where(r8 < k, pltpu.roll(prev, k, 0), body[0:8])
        return jnp.concatenate([head, body[8:]], axis=0)

    conv = (convw_ref[0:1, :] * shifted(2) + convw_ref[1:2, :] * shifted(1)
            + convw_ref[2:3, :] * gp + convb_ref[...])
    carry_ref[...] = gp[tm - 8:tm]
    act = (conv * jax.nn.sigmoid(conv) * val).astype(BF16)
    y = jnp.dot(act, wdown_ref[...], preferred_element_type=F32)
    x2 = x1 + gt_f * y
    o_ref[0] = x2 * _rms_scale(x2) * gfin_ref[...]


def _out_ffn(x, attn, ssm_sb, mod3, g_attn, w_out, g_ffn, w_up, conv_w, conv_b, w_down, g_final, tm):
    bsz, seq, d = x.shape
    aw = attn.shape[2]
    sw = ssm_sb.shape[1] // bsz
    dff = w_down.shape[0]
    const = lambda b, s: (0, 0)
    single = pl.Buffered(1)
    kern = functools.partial(_outffn_kernel, d=d, dff=dff)
    return pl.pallas_call(
        kern,
        out_shape=jax.ShapeDtypeStruct((bsz, seq, d), F32),
        grid=(bsz, seq // tm),
        in_specs=[pl.BlockSpec((1, tm, d), lambda b, s: (b, s, 0)),
                  pl.BlockSpec((1, tm, aw), lambda b, s: (b, s, 0)),
                  pl.BlockSpec((tm, sw), lambda b, s: (s, b)),
                  pl.BlockSpec((1, 1, mod3.shape[2]), lambda b, s: (b, 0, 0)),
                  pl.BlockSpec((1, aw), const),
                  pl.BlockSpec((aw + sw, d), const, pipeline_mode=single),
                  pl.BlockSpec((1, d), const),
                  pl.BlockSpec((d, 2 * dff), const, pipeline_mode=single),
                  pl.BlockSpec((CONV_WIDTH, dff), const),
                  pl.BlockSpec((1, dff), const),
                  pl.BlockSpec((dff, d), const, pipeline_mode=single),
                  pl.BlockSpec((1, d), const)],
        out_specs=pl.BlockSpec((1, tm, d), lambda b, s: (b, s, 0)),
        scratch_shapes=[pltpu.VMEM((8, dff), F32)],
        compiler_params=pltpu.CompilerParams(
            dimension_semantics=("arbitrary", "arbitrary"), vmem_limit_bytes=VMEM_LIMIT),
        name="out_ffn",
    )(x, attn, ssm_sb, mod3, g_attn, w_out, g_ffn, w_up, conv_w, conv_b, w_down, g_final)


def _block_diag(blocks):
    n, r, c = blocks.shape
    eye = jnp.eye(n, dtype=blocks.dtype)
    return (blocks[:, :, None, :] * eye[:, None, :, None]).reshape(n * r, n * c)


def _chunked_block_diag(blocks, per):
    g, r, c = blocks.shape
    return jax.vmap(_block_diag)(blocks.reshape(g // per, per, r, c))


def kernel(x, c, w_ada, b_ada, g_mix, w_in, b_fgate, a_re, a_im, log_dt, ssm_b_re, ssm_b_im,
           ssm_c_re, ssm_c_im, d_skip, w_glu, b_glu, g_attn_out, g_ssm_out, w_out, g_ffn, w_up,
           conv_w, conv_b, w_down, g_final):
    bsz, seq, d = x.shape
    depth = w_ada.shape[0]
    aw = N_HEADS * HEAD_DIM
    ngroups = a_re.shape[1]
    sw = ngroups * SSM_GROUP
    per = GROUPS_PER_CHUNK
    tm_in = min(TM_IN, seq)
    tq = min(TQ, seq)
    tm_ffn = min(TM_FFN, seq)
    tt = min(TT, seq)
    assert bsz % 8 == 0 and seq % tm_in == 0 and seq % tq == 0 and seq % tt == 0 and ngroups % per == 0

    for l in range(depth):
        mod3 = _modulation(c, w_ada[l], b_ada[l]).reshape(bsz, 1, N_MOD * d)

        w = w_in[l]
        w_cat = jnp.concatenate([w[:, 0:3 * aw], w[:, 3 * aw + N_HEADS:]], axis=1).astype(BF16)
        wf_t = w[:, 3 * aw:3 * aw + N_HEADS].T.astype(BF16)
        q, k, v, f, u_sb = _in_proj(x, mod3, g_mix[l].reshape(1, d), w_cat, wf_t,
                                    b_fgate[l].reshape(N_HEADS, 1), tm_in)

        attn = _attention(q, k, v, f, tq)

        l_re, l_im, bb_re, bb_im = _ssm_params(
            a_re[l], a_im[l], log_dt[l],
            ssm_b_re[l].transpose(0, 2, 1), ssm_b_im[l].transpose(0, 2, 1))
        bmat = jnp.concatenate([_chunked_block_diag(bb_re, per), _chunked_block_diag(bb_im, per)],
                               axis=-1).astype(BF16)
        cre = _chunked_block_diag(ssm_c_re[l].transpose(0, 2, 1), per).astype(BF16)
        cim = _chunked_block_diag(ssm_c_im[l].transpose(0, 2, 1), per).astype(BF16)
        wglu_bd = _block_diag(w_glu[l]).astype(BF16)
        ssm_sb = _ssm(u_sb.reshape(seq * bsz, sw),
                      l_re.reshape(ngroups // per, per * STATE_DIM),
                      l_im.reshape(ngroups // per, per * STATE_DIM),
                      bmat, cre, cim, d_skip[l].reshape(1, sw), wglu_bd, b_glu[l].reshape(1, sw),
                      g_ssm_out[l].reshape(1, sw), bsz, tt).reshape(seq, bsz * sw)

        assert depth == 1
        x = _out_ffn(x, attn, ssm_sb, mod3, g_attn_out[l].reshape(1, aw), w_out[l].astype(BF16),
                     g_ffn[l].reshape(1, d), w_up[l].astype(BF16), conv_w[l], conv_b[l].reshape(1, -1),
                     w_down[l].astype(BF16), g_final.reshape(1, d), tm_ffn)
    return x
```

```python
import functools
import math

import jax
import jax.numpy as jnp
from jax import lax
from jax.experimental import pallas as pl
from jax.experimental.pallas import tpu as pltpu

EPS = 1e-6
NEG_INF = -1e30
HEAD_DIM = 64
N_HEADS = 8
SSM_GROUP = 16
STATE_DIM = 64
CONV_WIDTH = 3
N_MOD = 6

LANES = 128
SUBLANES_BF16 = 16
GROUPS_PER_CHUNK = 8
VMEM_LIMIT = 56 * 1024 * 1024
TM_IN = 512
TQ = 512
ATTN_PREP = 512
TM_FFN = 512
TT = 32
LOG2E = math.log2(math.e)

BF16 = jnp.bfloat16
F32 = jnp.float32


def _split3(a):
    hi = a.astype(BF16)
    r1 = a - hi.astype(F32)
    mid = r1.astype(BF16)
    lo = (r1 - mid.astype(F32)).astype(BF16)
    return hi, mid, lo


def _rms_scale(x):
    return lax.rsqrt(jnp.mean(x * x, axis=-1, keepdims=True) + EPS)


def _mod_kernel(c_ref, w_ref, b_ref, o_ref):
    c = c_ref[...]
    s = c * jax.nn.sigmoid(c)
    w = w_ref[...]
    s_hi = s.astype(BF16)
    s_lo = (s - s_hi.astype(F32)).astype(BF16)
    w_hi = w.astype(BF16)
    w_lo = (w - w_hi.astype(F32)).astype(BF16)
    acc = jnp.dot(s_hi, w_hi, preferred_element_type=F32)
    acc += jnp.dot(s_lo, w_hi, preferred_element_type=F32)
    acc += jnp.dot(s_hi, w_lo, preferred_element_type=F32)
    o_ref[...] = acc + b_ref[...]


def _modulation(c, w_ada, b_ada):
    bsz, d = c.shape
    n = w_ada.shape[1]
    tn = 512
    return pl.pallas_call(
        _mod_kernel,
        out_shape=jax.ShapeDtypeStruct((bsz, n), F32),
        grid=(n // tn,),
        in_specs=[pl.BlockSpec((bsz, d), lambda j: (0, 0)),
                  pl.BlockSpec((d, tn), lambda j: (0, j)),
                  pl.BlockSpec((1, tn), lambda j: (0, j))],
        out_specs=pl.BlockSpec((bsz, tn), lambda j: (0, j)),
        compiler_params=pltpu.CompilerParams(dimension_semantics=("arbitrary",)),
        name="modulation",
    )(c, w_ada, b_ada.reshape(1, n))


def _inproj_kernel(x_ref, mod_ref, g_ref, w_ref, wft_ref, bf_ref, tri_ref,
                   q_ref, k_ref, v_ref, f_ref, u_ref, carry_ref, *, d, aw):
    si = pl.program_id(1)
    tm = x_ref.shape[1]

    @pl.when(si == 0)
    def _():
        carry_ref[...] = jnp.zeros_like(carry_ref)

    x = x_ref[0]
    shift = mod_ref[0, :, 0:d]
    scale = mod_ref[0, :, d:2 * d]
    h = x * _rms_scale(x) * g_ref[...]
    h = h * (1.0 + scale) + shift
    hb = h.astype(BF16)

    proj = jnp.dot(hb, w_ref[...], preferred_element_type=F32)
    q_ref[0] = (proj[:, 0:aw] * (HEAD_DIM ** -0.5 * LOG2E)).astype(BF16)
    k_ref[0] = proj[:, aw:2 * aw].astype(BF16)
    v_ref[0] = proj[:, 2 * aw:3 * aw].astype(BF16)
    u_ref[0] = proj[:, 3 * aw:].astype(BF16)

    z = lax.dot_general(wft_ref[...], hb, (((1,), (1,)), ((), ())),
                        preferred_element_type=F32) + bf_ref[...]
    logf = jnp.minimum(z, 0.0) - jnp.log1p(jnp.exp(-jnp.abs(z)))
    pieces = jnp.concatenate(_split3(logf), axis=0)
    cs = jnp.dot(pieces, tri_ref[...], preferred_element_type=F32)
    nh = logf.shape[0]
    cum = cs[0:nh] + cs[nh:2 * nh] + cs[2 * nh:3 * nh]
    cum = cum + carry_ref[:, LANES - 1:LANES]
    f_ref[0] = cum * LOG2E
    carry_ref[...] = cum[:, tm - LANES:tm]


def _in_proj(x, mod3, g_mix, w_cat, wf_t, b_f, tm):
    bsz, seq, d = x.shape
    ncat = w_cat.shape[1]
    aw = N_HEADS * HEAD_DIM
    sw = ncat - 3 * aw
    nh = wf_t.shape[0]
    tri = (jnp.arange(tm)[:, None] <= jnp.arange(tm)[None, :]).astype(BF16)
    const = lambda b, s: (0, 0)
    rows = lambda b, s: (b, s, 0)
    kern = functools.partial(_inproj_kernel, d=d, aw=aw)
    return pl.pallas_call(
        kern,
        out_shape=(jax.ShapeDtypeStruct((bsz, seq, aw), BF16),
                   jax.ShapeDtypeStruct((bsz, seq, aw), BF16),
                   jax.ShapeDtypeStruct((bsz, seq, aw), BF16),
                   jax.ShapeDtypeStruct((bsz, nh, seq), F32),
                   jax.ShapeDtypeStruct((bsz, seq, sw), BF16)),
        grid=(bsz, seq // tm),
        in_specs=[pl.BlockSpec((1, tm, d), rows),
                  pl.BlockSpec((1, 1, mod3.shape[2]), lambda b, s: (b, 0, 0)),
                  pl.BlockSpec((1, d), const),
                  pl.BlockSpec((d, ncat), const, pipeline_mode=pl.Buffered(1)),
                  pl.BlockSpec((nh, d), const),
                  pl.BlockSpec((nh, 1), const),
                  pl.BlockSpec((tm, tm), const, pipeline_mode=pl.Buffered(1))],
        out_specs=(pl.BlockSpec((1, tm, aw), rows),
                   pl.BlockSpec((1, tm, aw), rows),
                   pl.BlockSpec((1, tm, aw), rows),
                   pl.BlockSpec((1, nh, tm), lambda b, s: (b, 0, s)),
                   pl.BlockSpec((1, tm, sw), rows)),
        scratch_shapes=[pltpu.VMEM((nh, LANES), F32)],
        compiler_params=pltpu.CompilerParams(
            dimension_semantics=("arbitrary", "arbitrary"), vmem_limit_bytes=VMEM_LIMIT),
        name="in_proj",
    )(x, mod3, g_mix, w_cat, wf_t, b_f, tri)


def _attn_kernel(q_ref, k_ref, v_ref, f_ref, o_ref, qx_ref, kx_ref, vx_ref, m_ref, acc_ref, s_ref,
                 *, tq, prep):
    seq = k_ref.shape[1]
    data_lo = (0, HEAD_DIM)
    extra_lo = (HEAD_DIM, 0)

    def chunk(ci, _):
        r0 = pl.multiple_of(ci * prep, prep)
        lane = lax.broadcasted_iota(jnp.int32, (prep, LANES), 1)
        sub = lax.broadcasted_iota(jnp.int32, (LANES, prep), 0)
        qp = q_ref[0, pl.ds(r0, prep), :]
        kp = k_ref[0, pl.ds(r0, prep), :]
        vp = v_ref[0, pl.ds(r0, prep), :]
        for hh in range(2):
            e0 = extra_lo[hh]
            hi, mid, lo = [x.astype(F32) for x in _split3(f_ref[0, 0, hh:hh + 1, pl.ds(r0, prep)])]
            one = jnp.ones_like(hi)

            def columns(vals, e0=e0):
                out = jnp.zeros((LANES, prep), F32)
                for i, val in enumerate(vals):
                    out = jnp.where(sub == e0 + i, val, out)
                return out.T.astype(BF16)

            q_extra = columns([hi, mid, lo, one, one, one])
            k_extra = columns([one, one, one, -hi, -mid, -lo])
            in_head = (lane >= data_lo[hh]) & (lane < data_lo[hh] + HEAD_DIM)
            qx_ref[hh, pl.ds(r0, prep), :] = jnp.where(in_head, qp, q_extra)
            kx_ref[hh, pl.ds(r0, prep), :] = jnp.where(in_head, kp, k_extra)
            ones_col = jnp.where(lane == e0, 1.0, 0.0).astype(BF16)
            vx_ref[hh, pl.ds(r0, prep), :] = jnp.where(in_head, vp, ones_col)
        return 0
    lax.fori_loop(0, seq // prep, chunk, 0)

    row = lax.broadcasted_iota(jnp.int32, (tq, tq), 0)
    col = lax.broadcasted_iota(jnp.int32, (tq, tq), 1)
    causal = col <= row
    lane = lax.broadcasted_iota(jnp.int32, (tq, LANES), 1)

    def qk(q0, slot, j0):
        for hh in range(2):
            qx = qx_ref[hh, pl.ds(q0, tq), :]
            kb = kx_ref[hh, pl.ds(j0, tq), :]
            s_ref[slot, hh] = lax.dot_general(qx, kb, (((1,), (1,)), ((), ())),
                                              preferred_element_type=F32)

    def softmax_pv(par, slot, j0, mask=None):
        for hh in range(2):
            s = s_ref[slot, hh]
            if mask is not None:
                s = jnp.where(mask, s, NEG_INF)
            m_old = m_ref[par, hh]
            m_new = jnp.maximum(m_old, jnp.max(s, axis=-1, keepdims=True))
            alpha = jnp.exp2(m_old - m_new)
            p = jnp.exp2(s - jnp.concatenate([m_new] * (tq // LANES), axis=1)).astype(BF16)
            vb = vx_ref[hh, pl.ds(j0, tq), :]
            acc_ref[par, hh] = alpha * acc_ref[par, hh] + jnp.dot(p, vb, preferred_element_type=F32)
            m_ref[par, hh] = m_new

    def blk(j):
        return pl.multiple_of(j * tq, tq)

    nq = seq // tq
    slot = 0
    qk(0, slot, 0)
    for qi in range(nq):
        q0 = qi * tq
        par = qi % 2
        m_ref[par] = jnp.full(m_ref.shape[1:], NEG_INF, F32)
        acc_ref[par] = jnp.zeros(acc_ref.shape[1:], F32)
        npairs = qi // 2
        if npairs > 0:
            def body(t, _, q0=q0, par=par, slot=slot):
                qk(q0, 1 - slot, blk(2 * t + 1))
                softmax_pv(par, slot, blk(2 * t))
                qk(q0, slot, blk(2 * t + 2))
                softmax_pv(par, 1 - slot, blk(2 * t + 1))
                return 0
            lax.fori_loop(0, npairs, body, 0)
        if qi % 2 == 1:
            qk(q0, 1 - slot, q0)
            softmax_pv(par, slot, q0 - tq)
            slot = 1 - slot
        if qi + 1 < nq:
            qk(q0 + tq, 1 - slot, 0)
        softmax_pv(par, slot, q0, causal)
        slot = 1 - slot
        a0 = acc_ref[par, 0]
        a1 = acc_ref[par, 1]
        o0 = a0 / a0[:, extra_lo[0]:extra_lo[0] + 1]
        o1 = a1 / a1[:, extra_lo[1]:extra_lo[1] + 1]
        o_ref[0, q0:q0 + tq, :] = jnp.where(lane < HEAD_DIM, o0, o1).astype(BF16)


def _attention(q, k, v, f, tq):
    bsz, seq, aw = q.shape
    npair = aw // LANES
    f = f.reshape(bsz, npair, f.shape[1] // npair, seq)
    prep = min(ATTN_PREP, seq)
    kern = functools.partial(_attn_kernel, tq=tq, prep=prep)
    full = lambda b, h: (b, 0, h)
    return pl.pallas_call(
        kern,
        out_shape=jax.ShapeDtypeStruct((bsz, seq, aw), BF16),
        grid=(bsz, npair),
        in_specs=[pl.BlockSpec((1, seq, LANES), full),
                  pl.BlockSpec((1, seq, LANES), full),
                  pl.BlockSpec((1, seq, LANES), full),
                  pl.BlockSpec((1, 1, f.shape[2], seq), lambda b, h: (b, h, 0, 0))],
        out_specs=pl.BlockSpec((1, seq, LANES), full),
        scratch_shapes=[pltpu.VMEM((2, seq, LANES), BF16),
                        pltpu.VMEM((2, seq, LANES), BF16),
                        pltpu.VMEM((2, seq, LANES), BF16),
                        pltpu.VMEM((2, 2, tq, LANES), F32),
                        pltpu.VMEM((2, 2, tq, LANES), F32),
                        pltpu.VMEM((2, 2, tq, tq), F32)],
        compiler_params=pltpu.CompilerParams(
            dimension_semantics=("arbitrary", "arbitrary"), vmem_limit_bytes=VMEM_LIMIT),
        name="fox_attention",
    )(q, k, v, f)


def _ssm_param_kernel(are_ref, aim_ref, ldt_ref, bre_ref, bim_ref,
                      lre_ref, lim_ref, bbre_ref, bbim_ref):
    a_re = are_ref[...]
    a_im = aim_ref[...]
    dt = jnp.exp(ldt_ref[...])
    mag = jnp.exp(a_re * dt)
    ang = a_im * dt
    l_re = mag * jnp.cos(ang)
    l_im = mag * jnp.sin(ang)
    lre_ref[...] = l_re
    lim_ref[...] = l_im
    n_re = l_re - 1.0
    den = a_re * a_re + a_im * a_im
    c_re = (n_re * a_re + l_im * a_im) / den
    c_im = (l_im * a_re - n_re * a_im) / den
    c_re = c_re[:, None, :]
    c_im = c_im[:, None, :]
    b_re = bre_ref[...]
    b_im = bim_ref[...]
    bbre_ref[...] = c_re * b_re - c_im * b_im
    bbim_ref[...] = c_re * b_im + c_im * b_re


def _ssm_params(a_re, a_im, log_dt, b_re_t, b_im_t):
    g, p = a_re.shape
    c = b_re_t.shape[1]
    return pl.pallas_call(
        _ssm_param_kernel,
        out_shape=(jax.ShapeDtypeStruct((g, p), F32), jax.ShapeDtypeStruct((g, p), F32),
                   jax.ShapeDtypeStruct((g, c, p), F32), jax.ShapeDtypeStruct((g, c, p), F32)),
        name="ssm_params",
    )(a_re, a_im, log_dt.reshape(g, 1), b_re_t, b_im_t)


def _ssm_kernel(u_ref, perm_ref, permt_ref, lre_ref, lim_ref, bmat_ref, cre_ref, cim_ref, dskip_ref,
                wglu_ref, bglu_ref, gout_ref, o_ref, bu_ref, h_ref, sre_ref, sim_ref, *, nb, tt, th):
    nchunk = bmat_ref.shape[0]
    cw = bmat_ref.shape[2] // 2
    iw = bmat_ref.shape[1]
    w = u_ref.shape[2]

    @pl.when(pl.program_id(0) == 0)
    def _():
        sre_ref[...] = jnp.zeros_like(sre_ref)
        sim_ref[...] = jnp.zeros_like(sim_ref)

    parts = []
    for g in range(tt // th):
        ug = u_ref[:, g * th:(g + 1) * th, :].reshape(nb * th, w)
        parts.append(jnp.dot(perm_ref[...], ug, preferred_element_type=F32).astype(BF16))
    u = jnp.concatenate(parts, axis=0)

    def bu(c):
        base = 2 * cw * c
        bu_ref[:, base:base + 2 * cw] = jnp.dot(u[:, iw * c:iw * (c + 1)], bmat_ref[c],
                                                preferred_element_type=F32)

    def scan(c):
        base = 2 * cw * c
        a_re = jnp.broadcast_to(lre_ref[c:c + 1, :], (nb, cw))
        a_im = jnp.broadcast_to(lim_ref[c:c + 1, :], (nb, cw))
        h_re = sre_ref[:, cw * c:cw * (c + 1)]
        h_im = sim_ref[:, cw * c:cw * (c + 1)]
        for t in range(tt):
            r0 = t * nb
            b_re = bu_ref[r0:r0 + nb, base:base + cw]
            b_im = bu_ref[r0:r0 + nb, base + cw:base + 2 * cw]
            h_re, h_im = (a_re * h_re - a_im * h_im + b_re, a_re * h_im + a_im * h_re + b_im)
            h_ref[r0:r0 + nb, base:base + cw] = h_re.astype(BF16)
            h_ref[r0:r0 + nb, base + cw:base + 2 * cw] = h_im.astype(BF16)
        sre_ref[:, cw * c:cw * (c + 1)] = h_re
        sim_ref[:, cw * c:cw * (c + 1)] = h_im

    def readout(c):
        base = 2 * cw * c
        y = jnp.dot(h_ref[:, base:base + cw], cre_ref[c], preferred_element_type=F32)
        return y - jnp.dot(h_ref[:, base + cw:base + 2 * cw], cim_ref[c], preferred_element_type=F32)

    ys = []
    bu(0)
    for c in range(nchunk):
        if c + 1 < nchunk:
            bu(c + 1)
        scan(c)
        ys.append(readout(c))
    y = jnp.concatenate(ys, axis=-1) + dskip_ref[...] * u.astype(F32)
    z = jax.nn.gelu(y, approximate=True)
    gate = jnp.dot(z.astype(BF16), wglu_ref[...], preferred_element_type=F32) + bglu_ref[...]
    out = z * jax.nn.sigmoid(gate)
    out = (out * _rms_scale(out) * gout_ref[...]).astype(BF16)
    for g in range(tt // th):
        og = jnp.dot(permt_ref[...], out[g * th * nb:(g + 1) * th * nb], preferred_element_type=F32)
        o_ref[:, g * th:(g + 1) * th, :] = og.astype(BF16).reshape(nb, th, w)


def _ssm(u, l_re, l_im, bmat, cre, cim, dskip, wglu_bd, bglu, g_out, tt):
    nb, seq, w = u.shape
    th = SUBLANES_BF16
    rows = nb * tt
    nchunk, iw, cw2 = bmat.shape
    cw = cw2 // 2
    src = (jnp.arange(nb)[None, :] * th + jnp.arange(th)[:, None]).reshape(-1)
    perm = (src[:, None] == jnp.arange(nb * th)[None, :]).astype(BF16)
    const2 = lambda i: (0, 0)
    const3 = lambda i: (0, 0, 0)
    kern = functools.partial(_ssm_kernel, nb=nb, tt=tt, th=th)
    return pl.pallas_call(
        kern,
        out_shape=jax.ShapeDtypeStruct((nb, seq, w), BF16),
        grid=(seq // tt,),
        in_specs=[pl.BlockSpec((nb, tt, w), lambda i: (0, i, 0)),
                  pl.BlockSpec((nb * th, nb * th), const2),
                  pl.BlockSpec((nb * th, nb * th), const2),
                  pl.BlockSpec((nchunk, cw), const2),
                  pl.BlockSpec((nchunk, cw), const2),
                  pl.BlockSpec((nchunk, iw, cw2), const3),
                  pl.BlockSpec((nchunk, cw, iw), const3),
                  pl.BlockSpec((nchunk, cw, iw), const3),
                  pl.BlockSpec((1, w), const2),
                  pl.BlockSpec((w, w), const2),
                  pl.BlockSpec((1, w), const2),
                  pl.BlockSpec((1, w), const2)],
        out_specs=pl.BlockSpec((nb, tt, w), lambda i: (0, i, 0)),
        scratch_shapes=[pltpu.VMEM((rows, nchunk * cw2), F32),
                        pltpu.VMEM((rows, nchunk * cw2), BF16),
                        pltpu.VMEM((nb, nchunk * cw), F32),
                        pltpu.VMEM((nb, nchunk * cw), F32)],
        compiler_params=pltpu.CompilerParams(
            dimension_semantics=("arbitrary",), vmem_limit_bytes=VMEM_LIMIT),
        name="s5_ssm",
    )(u, perm, perm.T, l_re, l_im, bmat, cre, cim, dskip, wglu_bd, bglu, g_out)


def _outffn_kernel(x_ref, attn_ref, ssm_ref, mod_ref, gattn_ref, wout_ref, gffn_ref, wup_ref,
                   convw_ref, convb_ref, wdown_ref, gfin_ref, o_ref, carry_ref, *, d, dff):
    si = pl.program_id(1)
    tm = x_ref.shape[1]

    @pl.when(si == 0)
    def _():
        carry_ref[...] = jnp.zeros_like(carry_ref)

    gt_m = mod_ref[0, :, 2 * d:3 * d]
    sh_f = mod_ref[0, :, 3 * d:4 * d]
    sc_f = mod_ref[0, :, 4 * d:5 * d]
    gt_f = mod_ref[0, :, 5 * d:6 * d]

    a = attn_ref[0].astype(F32)
    an = (a * _rms_scale(a) * gattn_ref[...]).astype(BF16)
    mixin = jnp.concatenate([an, ssm_ref[0]], axis=-1)
    mix = jnp.dot(mixin, wout_ref[...], preferred_element_type=F32)
    x1 = x_ref[0] + gt_m * mix

    h = x1 * _rms_scale(x1) * gffn_ref[...]
    h = (h * (1.0 + sc_f) + sh_f).astype(BF16)
    up = jnp.dot(h, wup_ref[...], preferred_element_type=F32)
    gp = up[:, 0:dff]
    val = up[:, dff:2 * dff]

    prev = carry_ref[...]
    r8 = lax.broadcasted_iota(jnp.int32, prev.shape, 0)

    def shifted(k):
        body = pltpu.roll(gp, k, 0)
        head = jnp.where(r8 < k, pltpu.roll(prev, k, 0), body[0:8])
        return jnp.concatenate([head, body[8:]], axis=0)

    conv = (convw_ref[0:1, :] * shifted(2) + convw_ref[1:2, :] * shifted(1)
            + convw_ref[2:3, :] * gp + convb_ref[...])
    carry_ref[...] = gp[tm - 8:tm]
    act = (conv * jax.nn.sigmoid(conv) * val).astype(BF16)
    y = jnp.dot(act, wdown_ref[...], preferred_element_type=F32)
    x2 = x1 + gt_f * y
    o_ref[0] = x2 * _rms_scale(x2) * gfin_ref[...]


def _out_ffn(x, attn, ssm, mod3, g_attn, w_out, g_ffn, w_up, conv_w, conv_b, w_down, g_final, tm):
    bsz, seq, d = x.shape
    aw = attn.shape[2]
    sw = ssm.shape[2]
    dff = w_down.shape[0]
    const = lambda b, s: (0, 0)
    rows = lambda b, s: (b, s, 0)
    single = pl.Buffered(1)
    kern = functools.partial(_outffn_kernel, d=d, dff=dff)
    return pl.pallas_call(
        kern,
        out_shape=jax.ShapeDtypeStruct((bsz, seq, d), F32),
        grid=(bsz, seq // tm),
        in_specs=[pl.BlockSpec((1, tm, d), rows),
                  pl.BlockSpec((1, tm, aw), rows),
                  pl.BlockSpec((1, tm, sw), rows),
                  pl.BlockSpec((1, 1, mod3.shape[2]), lambda b, s: (b, 0, 0)),
                  pl.BlockSpec((1, aw), const),
                  pl.BlockSpec((aw + sw, d), const, pipeline_mode=single),
                  pl.BlockSpec((1, d), const),
                  pl.BlockSpec((d, 2 * dff), const, pipeline_mode=single),
                  pl.BlockSpec((CONV_WIDTH, dff), const),
                  pl.BlockSpec((1, dff), const),
                  pl.BlockSpec((dff, d), const, pipeline_mode=single),
                  pl.BlockSpec((1, d), const)],
        out_specs=pl.BlockSpec((1, tm, d), rows),
        scratch_shapes=[pltpu.VMEM((8, dff), F32)],
        compiler_params=pltpu.CompilerParams(
            dimension_semantics=("arbitrary", "arbitrary"), vmem_limit_bytes=VMEM_LIMIT),
        name="out_ffn",
    )(x, attn, ssm, mod3, g_attn, w_out, g_ffn, w_up, conv_w, conv_b, w_down, g_final)


def _block_diag(blocks):
    n, r, c = blocks.shape
    eye = jnp.eye(n, dtype=blocks.dtype)
    return (blocks[:, :, None, :] * eye[:, None, :, None]).reshape(n * r, n * c)


def _chunked_block_diag(blocks, per):
    g, r, c = blocks.shape
    return jax.vmap(_block_diag)(blocks.reshape(g // per, per, r, c))


def kernel(x, c, w_ada, b_ada, g_mix, w_in, b_fgate, a_re, a_im, log_dt, ssm_b_re, ssm_b_im,
           ssm_c_re, ssm_c_im, d_skip, w_glu, b_glu, g_attn_out, g_ssm_out, w_out, g_ffn, w_up,
           conv_w, conv_b, w_down, g_final):
    bsz, seq, d = x.shape
    depth = w_ada.shape[0]
    aw = N_HEADS * HEAD_DIM
    ngroups = a_re.shape[1]
    sw = ngroups * SSM_GROUP
    per = GROUPS_PER_CHUNK
    tm_in = min(TM_IN, seq)
    tq = min(TQ, seq)
    tm_ffn = min(TM_FFN, seq)
    tt = min(TT, seq)
    assert bsz % 8 == 0 and seq % tm_in == 0 and seq % tq == 0 and ngroups % per == 0
    assert seq % tt == 0 and tt % SUBLANES_BF16 == 0

    for l in range(depth):
        mod3 = _modulation(c, w_ada[l], b_ada[l]).reshape(bsz, 1, N_MOD * d)

        w = w_in[l]
        w_cat = jnp.concatenate([w[:, 0:3 * aw], w[:, 3 * aw + N_HEADS:]], axis=1).astype(BF16)
        wf_t = w[:, 3 * aw:3 * aw + N_HEADS].T.astype(BF16)
        q, k, v, f, u = _in_proj(x, mod3, g_mix[l].reshape(1, d), w_cat, wf_t,
                                 b_fgate[l].reshape(N_HEADS, 1), tm_in)

        attn = _attention(q, k, v, f, tq)

        l_re, l_im, bb_re, bb_im = _ssm_params(
            a_re[l], a_im[l], log_dt[l],
            ssm_b_re[l].transpose(0, 2, 1), ssm_b_im[l].transpose(0, 2, 1))
        bmat = jnp.concatenate([_chunked_block_diag(bb_re, per), _chunked_block_diag(bb_im, per)],
                               axis=-1).astype(BF16)
        cre = _chunked_block_diag(ssm_c_re[l].transpose(0, 2, 1), per).astype(BF16)
        cim = _chunked_block_diag(ssm_c_im[l].transpose(0, 2, 1), per).astype(BF16)
        wglu_bd = _block_diag(w_glu[l]).astype(BF16)
        ssm = _ssm(u, l_re.reshape(ngroups // per, per * STATE_DIM),
                   l_im.reshape(ngroups // per, per * STATE_DIM),
                   bmat, cre, cim, d_skip[l].reshape(1, sw), wglu_bd, b_glu[l].reshape(1, sw),
                   g_ssm_out[l].reshape(1, sw), tt)

        assert depth == 1
        x = _out_ffn(x, attn, ssm, mod3, g_attn_out[l].reshape(1, aw), w_out[l].astype(BF16),
                     g_ffn[l].reshape(1, d), w_up[l].astype(BF16), conv_w[l], conv_b[l].reshape(1, -1),
                     w_down[l].astype(BF16), g_final.reshape(1, d), tm_ffn)
    return x
```

```python
import functools
import math

import jax
import jax.numpy as jnp
from jax import lax
from jax.experimental import pallas as pl
from jax.experimental.pallas import tpu as pltpu

EPS = 1e-6
NEG_INF = -1e30
HEAD_DIM = 64
N_HEADS = 8
SSM_GROUP = 16
STATE_DIM = 64
CONV_WIDTH = 3
N_MOD = 6

LANES = 128
SUBLANES_BF16 = 16
GROUPS_PER_CHUNK = 8
VMEM_LIMIT = 56 * 1024 * 1024
TM_IN = 512
TQ = 512
ATTN_PREP = 512
TM_FFN = 512
TT = 32
LOG2E = math.log2(math.e)

BF16 = jnp.bfloat16
F32 = jnp.float32


def _split3(a):
    hi = a.astype(BF16)
    r1 = a - hi.astype(F32)
    mid = r1.astype(BF16)
    lo = (r1 - mid.astype(F32)).astype(BF16)
    return hi, mid, lo


def _rms_scale(x):
    return lax.rsqrt(jnp.mean(x * x, axis=-1, keepdims=True) + EPS)


def _mod_kernel(c_ref, w_ref, b_ref, o_ref):
    c = c_ref[...]
    s = c * jax.nn.sigmoid(c)
    w = w_ref[...]
    s_hi = s.astype(BF16)
    s_lo = (s - s_hi.astype(F32)).astype(BF16)
    w_hi = w.astype(BF16)
    w_lo = (w - w_hi.astype(F32)).astype(BF16)
    acc = jnp.dot(s_hi, w_hi, preferred_element_type=F32)
    acc += jnp.dot(s_lo, w_hi, preferred_element_type=F32)
    acc += jnp.dot(s_hi, w_lo, preferred_element_type=F32)
    o_ref[...] = acc + b_ref[...]


def _modulation(c, w_ada, b_ada):
    bsz, d = c.shape
    n = w_ada.shape[1]
    tn = 512
    return pl.pallas_call(
        _mod_kernel,
        out_shape=jax.ShapeDtypeStruct((bsz, n), F32),
        grid=(n // tn,),
        in_specs=[pl.BlockSpec((bsz, d), lambda j: (0, 0)),
                  pl.BlockSpec((d, tn), lambda j: (0, j)),
                  pl.BlockSpec((1, tn), lambda j: (0, j))],
        out_specs=pl.BlockSpec((bsz, tn), lambda j: (0, j)),
        compiler_params=pltpu.CompilerParams(dimension_semantics=("arbitrary",)),
        name="modulation",
    )(c, w_ada, b_ada.reshape(1, n))


def _inproj_kernel(x_ref, mod_ref, g_ref, w_ref, wft_ref, bf_ref, tri_ref,
                   q_ref, k_ref, v_ref, f_ref, u_ref, carry_ref, *, d, aw):
    si = pl.program_id(1)
    tm = x_ref.shape[1]

    @pl.when(si == 0)
    def _():
        carry_ref[...] = jnp.zeros_like(carry_ref)

    x = x_ref[0]
    shift = mod_ref[0, :, 0:d]
    scale = mod_ref[0, :, d:2 * d]
    h = x * _rms_scale(x) * g_ref[...]
    h = h * (1.0 + scale) + shift
    hb = h.astype(BF16)

    proj = jnp.dot(hb, w_ref[...], preferred_element_type=F32)
    q_ref[0] = (proj[:, 0:aw] * (HEAD_DIM ** -0.5 * LOG2E)).astype(BF16)
    k_ref[0] = proj[:, aw:2 * aw].astype(BF16)
    v_ref[0] = proj[:, 2 * aw:3 * aw].astype(BF16)
    u_ref[0] = proj[:, 3 * aw:].astype(BF16)

    z = lax.dot_general(wft_ref[...], hb, (((1,), (1,)), ((), ())),
                        preferred_element_type=F32) + bf_ref[...]
    logf = jnp.minimum(z, 0.0) - jnp.log1p(jnp.exp(-jnp.abs(z)))
    pieces = jnp.concatenate(_split3(logf), axis=0)
    cs = jnp.dot(pieces, tri_ref[...], preferred_element_type=F32)
    nh = logf.shape[0]
    cum = cs[0:nh] + cs[nh:2 * nh] + cs[2 * nh:3 * nh]
    cum = cum + carry_ref[:, LANES - 1:LANES]
    f_ref[0] = cum * LOG2E
    carry_ref[...] = cum[:, tm - LANES:tm]


def _in_proj(x, mod3, g_mix, w_cat, wf_t, b_f, tm):
    bsz, seq, d = x.shape
    ncat = w_cat.shape[1]
    aw = N_HEADS * HEAD_DIM
    sw = ncat - 3 * aw
    nh = wf_t.shape[0]
    tri = (jnp.arange(tm)[:, None] <= jnp.arange(tm)[None, :]).astype(BF16)
    const = lambda b, s: (0, 0)
    rows = lambda b, s: (b, s, 0)
    kern = functools.partial(_inproj_kernel, d=d, aw=aw)
    return pl.pallas_call(
        kern,
        out_shape=(jax.ShapeDtypeStruct((bsz, seq, aw), BF16),
                   jax.ShapeDtypeStruct((bsz, seq, aw), BF16),
                   jax.ShapeDtypeStruct((bsz, seq, aw), BF16),
                   jax.ShapeDtypeStruct((bsz, nh, seq), F32),
                   jax.ShapeDtypeStruct((bsz, seq, sw), BF16)),
        grid=(bsz, seq // tm),
        in_specs=[pl.BlockSpec((1, tm, d), rows),
                  pl.BlockSpec((1, 1, mod3.shape[2]), lambda b, s: (b, 0, 0)),
                  pl.BlockSpec((1, d), const),
                  pl.BlockSpec((d, ncat), const, pipeline_mode=pl.Buffered(1)),
                  pl.BlockSpec((nh, d), const),
                  pl.BlockSpec((nh, 1), const),
                  pl.BlockSpec((tm, tm), const, pipeline_mode=pl.Buffered(1))],
        out_specs=(pl.BlockSpec((1, tm, aw), rows),
                   pl.BlockSpec((1, tm, aw), rows),
                   pl.BlockSpec((1, tm, aw), rows),
                   pl.BlockSpec((1, nh, tm), lambda b, s: (b, 0, s)),
                   pl.BlockSpec((1, tm, sw), rows)),
        scratch_shapes=[pltpu.VMEM((nh, LANES), F32)],
        compiler_params=pltpu.CompilerParams(
            dimension_semantics=("arbitrary", "arbitrary"), vmem_limit_bytes=VMEM_LIMIT),
        name="in_proj",
    )(x, mod3, g_mix, w_cat, wf_t, b_f, tri)


def _attn_kernel(q_ref, k_ref, v_ref, f_ref, o_ref, qx_ref, kx_ref, vx_ref, m_ref, acc_ref, s_ref,
                 *, tq, prep):
    seq = k_ref.shape[1]
    data_lo = (0, HEAD_DIM)
    extra_lo = (HEAD_DIM, 0)

    def chunk(ci, _):
        r0 = pl.multiple_of(ci * prep, prep)
        lane = lax.broadcasted_iota(jnp.int32, (prep, LANES), 1)
        sub = lax.broadcasted_iota(jnp.int32, (LANES, prep), 0)
        qp = q_ref[0, pl.ds(r0, prep), :]
        kp = k_ref[0, pl.ds(r0, prep), :]
        vp = v_ref[0, pl.ds(r0, prep), :]
        for hh in range(2):
            e0 = extra_lo[hh]
            hi, mid, lo = [x.astype(F32) for x in _split3(f_ref[0, 0, hh:hh + 1, pl.ds(r0, prep)])]
            one = jnp.ones_like(hi)

            def columns(vals, e0=e0):
                out = jnp.zeros((LANES, prep), F32)
                for i, val in enumerate(vals):
                    out = jnp.where(sub == e0 + i, val, out)
                return out.T.astype(BF16)

            q_extra = columns([hi, mid, lo, one, one, one])
            k_extra = columns([one, one, one, -hi, -mid, -lo])
            in_head = (lane >= data_lo[hh]) & (lane < data_lo[hh] + HEAD_DIM)
            qx_ref[hh, pl.ds(r0, prep), :] = jnp.where(in_head, qp, q_extra)
            kx_ref[hh, pl.ds(r0, prep), :] = jnp.where(in_head, kp, k_extra)
            ones_col = jnp.where(lane == e0, 1.0, 0.0).astype(BF16)
            vx_ref[hh, pl.ds(r0, prep), :] = jnp.where(in_head, vp, ones_col)
        return 0
    lax.fori_loop(0, seq // prep, chunk, 0)

    row = lax.broadcasted_iota(jnp.int32, (tq, tq), 0)
    col = lax.broadcasted_iota(jnp.int32, (tq, tq), 1)
    causal = col <= row
    lane = lax.broadcasted_iota(jnp.int32, (tq, LANES), 1)

    def qk(q0, slot, j0, width):
        for hh in range(2):
            qx = qx_ref[hh, q0:q0 + tq, :]
            kb = kx_ref[hh, j0:j0 + width, :]
            s_ref[slot, hh, :, 0:width] = lax.dot_general(qx, kb, (((1,), (1,)), ((), ())),
                                                          preferred_element_type=F32)

    def softmax_pv(par, slot, j0, width, masked):
        for hh in range(2):
            s = s_ref[slot, hh, :, 0:width]
            if masked:
                tail = jnp.where(causal, s[:, width - tq:], NEG_INF)
                s = tail if width == tq else jnp.concatenate([s[:, 0:width - tq], tail], axis=1)
            m_old = m_ref[par, hh]
            m_new = jnp.maximum(m_old, jnp.max(s, axis=-1, keepdims=True))
            alpha = jnp.exp2(m_old - m_new)
            p = jnp.exp2(s - jnp.concatenate([m_new] * (width // LANES), axis=1)).astype(BF16)
            vb = vx_ref[hh, j0:j0 + width, :]
            acc_ref[par, hh] = alpha * acc_ref[par, hh] + jnp.dot(p, vb, preferred_element_type=F32)
            m_ref[par, hh] = m_new

    nq = seq // tq
    steps = []
    for qi in range(nq):
        nkeys = (qi + 1) * tq
        j0 = 0
        while j0 < nkeys:
            width = min(2 * tq, nkeys - j0)
            steps.append((qi, j0, width, j0 + width == nkeys))
            j0 += width
    qk(0, 0, steps[0][1], steps[0][2])
    for i, (qi, j0, width, last) in enumerate(steps):
        slot = i % 2
        par = qi % 2
        q0 = qi * tq
        if j0 == 0:
            m_ref[par] = jnp.full(m_ref.shape[1:], NEG_INF, F32)
            acc_ref[par] = jnp.zeros(acc_ref.shape[1:], F32)
        if i + 1 < len(steps):
            nqi, nj0, nwidth, _ = steps[i + 1]
            qk(nqi * tq, 1 - slot, nj0, nwidth)
        softmax_pv(par, slot, j0, width, last)
        if last:
            a0 = acc_ref[par, 0]
            a1 = acc_ref[par, 1]
            o0 = a0 / a0[:, extra_lo[0]:extra_lo[0] + 1]
            o1 = a1 / a1[:, extra_lo[1]:extra_lo[1] + 1]
            o_ref[0, q0:q0 + tq, :] = jnp.where(lane < HEAD_DIM, o0, o1).astype(BF16)


def _attention(q, k, v, f, tq):
    bsz, seq, aw = q.shape
    npair = aw // LANES
    f = f.reshape(bsz, npair, f.shape[1] // npair, seq)
    prep = min(ATTN_PREP, seq)
    kern = functools.partial(_attn_kernel, tq=tq, prep=prep)
    full = lambda b, h: (b, 0, h)
    return pl.pallas_call(
        kern,
        out_shape=jax.ShapeDtypeStruct((bsz, seq, aw), BF16),
        grid=(bsz, npair),
        in_specs=[pl.BlockSpec((1, seq, LANES), full),
                  pl.BlockSpec((1, seq, LANES), full),
                  pl.BlockSpec((1, seq, LANES), full),
                  pl.BlockSpec((1, 1, f.shape[2], seq), lambda b, h: (b, h, 0, 0))],
        out_specs=pl.BlockSpec((1, seq, LANES), full),
        scratch_shapes=[pltpu.VMEM((2, seq, LANES), BF16),
                        pltpu.VMEM((2, seq, LANES), BF16),
                        pltpu.VMEM((2, seq, LANES), BF16),
                        pltpu.VMEM((2, 2, tq, LANES), F32),
                        pltpu.VMEM((2, 2, tq, LANES), F32),
                        pltpu.VMEM((2, 2, tq, 2 * tq), F32)],
        compiler_params=pltpu.CompilerParams(
            dimension_semantics=("arbitrary", "arbitrary"), vmem_limit_bytes=VMEM_LIMIT),
        name="fox_attention",
    )(q, k, v, f)


def _ssm_param_kernel(are_ref, aim_ref, ldt_ref, bre_ref, bim_ref,
                      lre_ref, lim_ref, bbre_ref, bbim_ref):
    a_re = are_ref[...]
    a_im = aim_ref[...]
    dt = jnp.exp(ldt_ref[...])
    mag = jnp.exp(a_re * dt)
    ang = a_im * dt
    l_re = mag * jnp.cos(ang)
    l_im = mag * jnp.sin(ang)
    lre_ref[...] = l_re
    lim_ref[...] = l_im
    n_re = l_re - 1.0
    den = a_re * a_re + a_im * a_im
    c_re = (n_re * a_re + l_im * a_im) / den
    c_im = (l_im * a_re - n_re * a_im) / den
    c_re = c_re[:, None, :]
    c_im = c_im[:, None, :]
    b_re = bre_ref[...]
    b_im = bim_ref[...]
    bbre_ref[...] = c_re * b_re - c_im * b_im
    bbim_ref[...] = c_re * b_im + c_im * b_re


def _ssm_params(a_re, a_im, log_dt, b_re_t, b_im_t):
    g, p = a_re.shape
    c = b_re_t.shape[1]
    return pl.pallas_call(
        _ssm_param_kernel,
        out_shape=(jax.ShapeDtypeStruct((g, p), F32), jax.ShapeDtypeStruct((g, p), F32),
                   jax.ShapeDtypeStruct((g, c, p), F32), jax.ShapeDtypeStruct((g, c, p), F32)),
        name="ssm_params",
    )(a_re, a_im, log_dt.reshape(g, 1), b_re_t, b_im_t)


def _ssm_kernel(u_ref, perm_ref, permt_ref, lre_ref, lim_ref, bmat_ref, cre_ref, cim_ref, dskip_ref,
                wglu_ref, bglu_ref, gout_ref, o_ref, bu_ref, h_ref, sre_ref, sim_ref, *, nb, tt, th):
    nchunk = bmat_ref.shape[0]
    cw = bmat_ref.shape[2] // 2
    iw = bmat_ref.shape[1]
    w = u_ref.shape[2]

    @pl.when(pl.program_id(0) == 0)
    def _():
        sre_ref[...] = jnp.zeros_like(sre_ref)
        sim_ref[...] = jnp.zeros_like(sim_ref)

    parts = []
    for g in range(tt // th):
        ug = u_ref[:, g * th:(g + 1) * th, :].reshape(nb * th, w)
        parts.append(jnp.dot(perm_ref[...], ug, preferred_element_type=F32).astype(BF16))
    u = jnp.concatenate(parts, axis=0)

    def bu(c):
        base = 2 * cw * c
        bu_ref[:, base:base + 2 * cw] = jnp.dot(u[:, iw * c:iw * (c + 1)], bmat_ref[c],
                                                preferred_element_type=F32)

    def scan(c):
        base = 2 * cw * c
        a_re = jnp.broadcast_to(lre_ref[c:c + 1, :], (nb, cw))
        a_im = jnp.broadcast_to(lim_ref[c:c + 1, :], (nb, cw))
        h_re = sre_ref[:, cw * c:cw * (c + 1)]
        h_im = sim_ref[:, cw * c:cw * (c + 1)]
        for t in range(tt):
            r0 = t * nb
            b_re = bu_ref[r0:r0 + nb, base:base + cw]
            b_im = bu_ref[r0:r0 + nb, base + cw:base + 2 * cw]
            h_re, h_im = (a_re * h_re - a_im * h_im + b_re, a_re * h_im + a_im * h_re + b_im)
            h_ref[r0:r0 + nb, base:base + cw] = h_re.astype(BF16)
            h_ref[r0:r0 + nb, base + cw:base + 2 * cw] = h_im.astype(BF16)
        sre_ref[:, cw * c:cw * (c + 1)] = h_re
        sim_ref[:, cw * c:cw * (c + 1)] = h_im

    def readout(c):
        base = 2 * cw * c
        y = jnp.dot(h_ref[:, base:base + cw], cre_ref[c], preferred_element_type=F32)
        return y - jnp.dot(h_ref[:, base + cw:base + 2 * cw], cim_ref[c], preferred_element_type=F32)

    ys = []
    bu(0)
    for c in range(nchunk):
        if c + 1 < nchunk:
            bu(c + 1)
        scan(c)
        ys.append(readout(c))
    y = jnp.concatenate(ys, axis=-1) + dskip_ref[...] * u.astype(F32)
    z = jax.nn.gelu(y, approximate=True)
    gate = jnp.dot(z.astype(BF16), wglu_ref[...], preferred_element_type=F32) + bglu_ref[...]
    out = z * jax.nn.sigmoid(gate)
    out = (out * _rms_scale(out) * gout_ref[...]).astype(BF16)
    for g in range(tt // th):
        og = jnp.dot(permt_ref[...], out[g * th * nb:(g + 1) * th * nb], preferred_element_type=F32)
        o_ref[:, g * th:(g + 1) * th, :] = og.astype(BF16).reshape(nb, th, w)


def _ssm(u, l_re, l_im, bmat, cre, cim, dskip, wglu_bd, bglu, g_out, tt):
    nb, seq, w = u.shape
    th = SUBLANES_BF16
    rows = nb * tt
    nchunk, iw, cw2 = bmat.shape
    cw = cw2 // 2
    src = (jnp.arange(nb)[None, :] * th + jnp.arange(th)[:, None]).reshape(-1)
    perm = (src[:, None] == jnp.arange(nb * th)[None, :]).astype(BF16)
    const2 = lambda i: (0, 0)
    const3 = lambda i: (0, 0, 0)
    kern = functools.partial(_ssm_kernel, nb=nb, tt=tt, th=th)
    return pl.pallas_call(
        kern,
        out_shape=jax.ShapeDtypeStruct((nb, seq, w), BF16),
        grid=(seq // tt,),
        in_specs=[pl.BlockSpec((nb, tt, w), lambda i: (0, i, 0)),
                  pl.BlockSpec((nb * th, nb * th), const2),
                  pl.BlockSpec((nb * th, nb * th), const2),
                  pl.BlockSpec((nchunk, cw), const2),
                  pl.BlockSpec((nchunk, cw), const2),
                  pl.BlockSpec((nchunk, iw, cw2), const3),
                  pl.BlockSpec((nchunk, cw, iw), const3),
                  pl.BlockSpec((nchunk, cw, iw), const3),
                  pl.BlockSpec((1, w), const2),
                  pl.BlockSpec((w, w), const2),
                  pl.BlockSpec((1, w), const2),
                  pl.BlockSpec((1, w), const2)],
        out_specs=pl.BlockSpec((nb, tt, w), lambda i: (0, i, 0)),
        scratch_shapes=[pltpu.VMEM((rows, nchunk * cw2), F32),
                        pltpu.VMEM((rows, nchunk * cw2), BF16),
                        pltpu.VMEM((nb, nchunk * cw), F32),
                        pltpu.VMEM((nb, nchunk * cw), F32)],
        compiler_params=pltpu.CompilerParams(
            dimension_semantics=("arbitrary",), vmem_limit_bytes=VMEM_LIMIT),
        name="s5_ssm",
    )(u, perm, perm.T, l_re, l_im, bmat, cre, cim, dskip, wglu_bd, bglu, g_out)


def _outffn_kernel(x_ref, attn_ref, ssm_ref, mod_ref, gattn_ref, wout_ref, gffn_ref, wup_ref,
                   convw_ref, convb_ref, wdown_ref, gfin_ref, o_ref, carry_ref, *, d, dff):
    si = pl.program_id(1)
    tm = x_ref.shape[1]

    @pl.when(si == 0)
    def _():
        carry_ref[...] = jnp.zeros_like(carry_ref)

    gt_m = mod_ref[0, :, 2 * d:3 * d]
    sh_f = mod_ref[0, :, 3 * d:4 * d]
    sc_f = mod_ref[0, :, 4 * d:5 * d]
    gt_f = mod_ref[0, :, 5 * d:6 * d]

    a = attn_ref[0].astype(F32)
    an = (a * _rms_scale(a) * gattn_ref[...]).astype(BF16)
    mixin = jnp.concatenate([an, ssm_ref[0]], axis=-1)
    mix = jnp.dot(mixin, wout_ref[...], preferred_element_type=F32)
    x1 = x_ref[0] + gt_m * mix

    h = x1 * _rms_scale(x1) * gffn_ref[...]
    h = (h * (1.0 + sc_f) + sh_f).astype(BF16)
    up = jnp.dot(h, wup_ref[...], preferred_element_type=F32)
    gp = up[:, 0:dff]
    val = up[:, dff:2 * dff]

    prev = carry_ref[...]
    r8 = lax.broadcasted_iota(jnp.int32, prev.shape, 0)

    def shifted(k):
        body = pltpu.roll(gp, k, 0)
        head = jnp.where(r8 < k, pltpu.roll(prev, k, 0), body[0:8])
        return jnp.concatenate([head, body[8:]], axis=0)

    conv = (convw_ref[0:1, :] * shifted(2) + convw_ref[1:2, :] * shifted(1)
            + convw_ref[2:3, :] * gp + convb_ref[...])
    carry_ref[...] = gp[tm - 8:tm]
    act = (conv * jax.nn.sigmoid(conv) * val).astype(BF16)
    y = jnp.dot(act, wdown_ref[...], preferred_element_type=F32)
    x2 = x1 + gt_f * y
    o_ref[0] = x2 * _rms_scale(x2) * gfin_ref[...]


def _out_ffn(x, attn, ssm, mod3, g_attn, w_out, g_ffn, w_up, conv_w, conv_b, w_down, g_final, tm):
    bsz, seq, d = x.shape
    aw = attn.shape[2]
    sw = ssm.shape[2]
    dff = w_down.shape[0]
    const = lambda b, s: (0, 0)
    rows = lambda b, s: (b, s, 0)
    single = pl.Buffered(1)
    kern = functools.partial(_outffn_kernel, d=d, dff=dff)
    return pl.pallas_call(
        kern,
        out_shape=jax.ShapeDtypeStruct((bsz, seq, d), F32),
        grid=(bsz, seq // tm),
        in_specs=[pl.BlockSpec((1, tm, d), rows),
                  pl.BlockSpec((1, tm, aw), rows),
                  pl.BlockSpec((1, tm, sw), rows),
                  pl.BlockSpec((1, 1, mod3.shape[2]), lambda b, s: (b, 0, 0)),
                  pl.BlockSpec((1, aw), const),
                  pl.BlockSpec((aw + sw, d), const, pipeline_mode=single),
                  pl.BlockSpec((1, d), const),
                  pl.BlockSpec((d, 2 * dff), const, pipeline_mode=single),
                  pl.BlockSpec((CONV_WIDTH, dff), const),
                  pl.BlockSpec((1, dff), const),
                  pl.BlockSpec((dff, d), const, pipeline_mode=single),
                  pl.BlockSpec((1, d), const)],
        out_specs=pl.BlockSpec((1, tm, d), rows),
        scratch_shapes=[pltpu.VMEM((8, dff), F32)],
        compiler_params=pltpu.CompilerParams(
            dimension_semantics=("arbitrary", "arbitrary"), vmem_limit_bytes=VMEM_LIMIT),
        name="out_ffn",
    )(x, attn, ssm, mod3, g_attn, w_out, g_ffn, w_up, conv_w, conv_b, w_down, g_final)


def _block_diag(blocks):
    n, r, c = blocks.shape
    eye = jnp.eye(n, dtype=blocks.dtype)
    return (blocks[:, :, None, :] * eye[:, None, :, None]).reshape(n * r, n * c)


def _chunked_block_diag(blocks, per):
    g, r, c = blocks.shape
    return jax.vmap(_block_diag)(blocks.reshape(g // per, per, r, c))


def kernel(x, c, w_ada, b_ada, g_mix, w_in, b_fgate, a_re, a_im, log_dt, ssm_b_re, ssm_b_im,
           ssm_c_re, ssm_c_im, d_skip, w_glu, b_glu, g_attn_out, g_ssm_out, w_out, g_ffn, w_up,
           conv_w, conv_b, w_down, g_final):
    bsz, seq, d = x.shape
    depth = w_ada.shape[0]
    aw = N_HEADS * HEAD_DIM
    ngroups = a_re.shape[1]
    sw = ngroups * SSM_GROUP
    per = GROUPS_PER_CHUNK
    tm_in = min(TM_IN, seq)
    tq = min(TQ, seq)
    tm_ffn = min(TM_FFN, seq)
    tt = min(TT, seq)
    assert bsz % 8 == 0 and seq % tm_in == 0 and seq % tq == 0 and ngroups % per == 0
    assert seq % tt == 0 and tt % SUBLANES_BF16 == 0

    for l in range(depth):
        mod3 = _modulation(c, w_ada[l], b_ada[l]).reshape(bsz, 1, N_MOD * d)

        w = w_in[l]
        w_cat = jnp.concatenate([w[:, 0:3 * aw], w[:, 3 * aw + N_HEADS:]], axis=1).astype(BF16)
        wf_t = w[:, 3 * aw:3 * aw + N_HEADS].T.astype(BF16)
        q, k, v, f, u = _in_proj(x, mod3, g_mix[l].reshape(1, d), w_cat, wf_t,
                                 b_fgate[l].reshape(N_HEADS, 1), tm_in)

        attn = _attention(q, k, v, f, tq)

        l_re, l_im, bb_re, bb_im = _ssm_params(
            a_re[l], a_im[l], log_dt[l],
            ssm_b_re[l].transpose(0, 2, 1), ssm_b_im[l].transpose(0, 2, 1))
        bmat = jnp.concatenate([_chunked_block_diag(bb_re, per), _chunked_block_diag(bb_im, per)],
                               axis=-1).astype(BF16)
        cre = _chunked_block_diag(ssm_c_re[l].transpose(0, 2, 1), per).astype(BF16)
        cim = _chunked_block_diag(ssm_c_im[l].transpose(0, 2, 1), per).astype(BF16)
        wglu_bd = _block_diag(w_glu[l]).astype(BF16)
        ssm = _ssm(u, l_re.reshape(ngroups // per, per * STATE_DIM),
                   l_im.reshape(ngroups // per, per * STATE_DIM),
                   bmat, cre, cim, d_skip[l].reshape(1, sw), wglu_bd, b_glu[l].reshape(1, sw),
                   g_ssm_out[l].reshape(1, sw), tt)

        assert depth == 1
        x = _out_ffn(x, attn, ssm, mod3, g_attn_out[l].reshape(1, aw), w_out[l].astype(BF16),
                     g_ffn[l].reshape(1, d), w_up[l].astype(BF16), conv_w[l], conv_b[l].reshape(1, -1),
                     w_down[l].astype(BF16), g_final.reshape(1, d), tm_ffn)
    return x
```

```python
import functools
import math

import jax
import jax.numpy as jnp
from jax import lax
from jax.experimental import pallas as pl
from jax.experimental.pallas import tpu as pltpu

EPS = 1e-6
NEG_INF = -1e30
HEAD_DIM = 64
N_HEADS = 8
SSM_GROUP = 16
STATE_DIM = 64
CONV_WIDTH = 3
N_MOD = 6

LANES = 128
SUBLANES = 8
SUBLANES_BF16 = 16
GROUPS_PER_CHUNK = 8
VMEM_LIMIT = 56 * 1024 * 1024
TM_IN = 1024
TQ = 512
ATTN_PREP = 512
TM_FFN = 512
TT = 32
LOG2E = math.log2(math.e)

BF16 = jnp.bfloat16
F32 = jnp.float32


def _split3(a):
    hi = a.astype(BF16)
    r1 = a - hi.astype(F32)
    mid = r1.astype(BF16)
    lo = (r1 - mid.astype(F32)).astype(BF16)
    return hi, mid, lo


def _rms_scale(x):
    return lax.rsqrt(jnp.mean(x * x, axis=-1, keepdims=True) + EPS)


def _mod_kernel(c_ref, w_ref, b_ref, o_ref):
    c = c_ref[...]
    s = c * jax.nn.sigmoid(c)
    w = w_ref[...]
    s_hi = s.astype(BF16)
    s_lo = (s - s_hi.astype(F32)).astype(BF16)
    w_hi = w.astype(BF16)
    w_lo = (w - w_hi.astype(F32)).astype(BF16)
    acc = jnp.dot(s_hi, w_hi, preferred_element_type=F32)
    acc += jnp.dot(s_lo, w_hi, preferred_element_type=F32)
    acc += jnp.dot(s_hi, w_lo, preferred_element_type=F32)
    o_ref[...] = acc + b_ref[...]


def _modulation(c, w_ada, b_ada):
    bsz, d = c.shape
    n = w_ada.shape[1]
    tn = 512
    return pl.pallas_call(
        _mod_kernel,
        out_shape=jax.ShapeDtypeStruct((bsz, n), F32),
        grid=(n // tn,),
        in_specs=[pl.BlockSpec((bsz, d), lambda j: (0, 0)),
                  pl.BlockSpec((d, tn), lambda j: (0, j)),
                  pl.BlockSpec((1, tn), lambda j: (0, j))],
        out_specs=pl.BlockSpec((bsz, tn), lambda j: (0, j)),
        compiler_params=pltpu.CompilerParams(dimension_semantics=("arbitrary",)),
        name="modulation",
    )(c, w_ada, b_ada.reshape(1, n))


def _inproj_kernel(x_ref, mod_ref, g_ref, w_ref, wft_ref, bf_ref, tri_ref,
                   q_ref, k_ref, v_ref, f_ref, u_ref, carry_ref, *, d, aw):
    si = pl.program_id(1)
    tm = x_ref.shape[1]

    @pl.when(si == 0)
    def _():
        carry_ref[...] = jnp.zeros_like(carry_ref)

    x = x_ref[0]
    shift = mod_ref[0, :, 0:d]
    scale = mod_ref[0, :, d:2 * d]
    h = x * _rms_scale(x) * g_ref[...]
    h = h * (1.0 + scale) + shift
    hb = h.astype(BF16)

    proj = jnp.dot(hb, w_ref[...], preferred_element_type=F32)
    q_ref[0] = (proj[:, 0:aw] * (HEAD_DIM ** -0.5 * LOG2E)).astype(BF16)
    k_ref[0] = proj[:, aw:2 * aw].astype(BF16)
    v_ref[0] = proj[:, 2 * aw:3 * aw].astype(BF16)
    u_ref[0] = proj[:, 3 * aw:].astype(BF16)

    z = lax.dot_general(wft_ref[...], hb, (((1,), (1,)), ((), ())),
                        preferred_element_type=F32) + bf_ref[...]
    logf = jnp.minimum(z, 0.0) - jnp.log1p(jnp.exp(-jnp.abs(z)))
    pieces = jnp.concatenate(_split3(logf), axis=0)
    cs = jnp.dot(pieces, tri_ref[...], preferred_element_type=F32)
    nh = logf.shape[0]
    cum = cs[0:nh] + cs[nh:2 * nh] + cs[2 * nh:3 * nh]
    cum = cum + carry_ref[:, LANES - 1:LANES]
    f_ref[0] = cum * LOG2E
    carry_ref[...] = cum[:, tm - LANES:tm]


def _in_proj(x, mod3, g_mix, w_cat, wf_t, b_f, tm):
    bsz, seq, d = x.shape
    ncat = w_cat.shape[1]
    aw = N_HEADS * HEAD_DIM
    sw = ncat - 3 * aw
    nh = wf_t.shape[0]
    tri = (jnp.arange(tm)[:, None] <= jnp.arange(tm)[None, :]).astype(BF16)
    const = lambda b, s: (0, 0)
    rows = lambda b, s: (b, s, 0)
    kern = functools.partial(_inproj_kernel, d=d, aw=aw)
    return pl.pallas_call(
        kern,
        out_shape=(jax.ShapeDtypeStruct((bsz, seq, aw), BF16),
                   jax.ShapeDtypeStruct((bsz, seq, aw), BF16),
                   jax.ShapeDtypeStruct((bsz, seq, aw), BF16),
                   jax.ShapeDtypeStruct((bsz, nh, seq), F32),
                   jax.ShapeDtypeStruct((bsz, seq, sw), BF16)),
        grid=(bsz, seq // tm),
        in_specs=[pl.BlockSpec((1, tm, d), rows),
                  pl.BlockSpec((1, 1, mod3.shape[2]), lambda b, s: (b, 0, 0)),
                  pl.BlockSpec((1, d), const),
                  pl.BlockSpec((d, ncat), const, pipeline_mode=pl.Buffered(1)),
                  pl.BlockSpec((nh, d), const),
                  pl.BlockSpec((nh, 1), const),
                  pl.BlockSpec((tm, tm), const, pipeline_mode=pl.Buffered(1))],
        out_specs=(pl.BlockSpec((1, tm, aw), rows),
                   pl.BlockSpec((1, tm, aw), rows),
                   pl.BlockSpec((1, tm, aw), rows),
                   pl.BlockSpec((1, nh, tm), lambda b, s: (b, 0, s)),
                   pl.BlockSpec((1, tm, sw), rows)),
        scratch_shapes=[pltpu.VMEM((nh, LANES), F32)],
        compiler_params=pltpu.CompilerParams(
            dimension_semantics=("arbitrary", "arbitrary"), vmem_limit_bytes=VMEM_LIMIT),
        name="in_proj",
    )(x, mod3, g_mix, w_cat, wf_t, b_f, tri)


def _attn_kernel(q_ref, k_ref, v_ref, f_ref, o_ref, qx_ref, kx_ref, vx_ref, m_ref, acc_ref, s_ref,
                 *, tq, prep):
    seq = k_ref.shape[1]
    data_lo = (0, HEAD_DIM)
    extra_lo = (HEAD_DIM, 0)

    def chunk(ci, _):
        r0 = pl.multiple_of(ci * prep, prep)
        lane = lax.broadcasted_iota(jnp.int32, (prep, LANES), 1)
        sub = lax.broadcasted_iota(jnp.int32, (SUBLANES, prep), 0)
        qp = q_ref[0, pl.ds(r0, prep), :]
        kp = k_ref[0, pl.ds(r0, prep), :]
        vp = v_ref[0, pl.ds(r0, prep), :]
        for hh in range(2):
            e0 = extra_lo[hh]
            hi, mid, lo = [x.astype(F32) for x in _split3(f_ref[0, 0, hh:hh + 1, pl.ds(r0, prep)])]
            one = jnp.ones_like(hi)

            def columns(vals, e0=e0):
                tile = jnp.zeros((SUBLANES, prep), F32)
                for i, val in enumerate(vals):
                    tile = jnp.where(sub == i, val, tile)
                rows = [jnp.zeros((e0, prep), F32)] if e0 else []
                rows += [tile, jnp.zeros((LANES - e0 - SUBLANES, prep), F32)]
                return jnp.concatenate(rows, axis=0).T.astype(BF16)

            q_extra = columns([hi, mid, lo, one, one, one])
            k_extra = columns([one, one, one, -hi, -mid, -lo])
            in_head = (lane >= data_lo[hh]) & (lane < data_lo[hh] + HEAD_DIM)
            qx_ref[hh, pl.ds(r0, prep), :] = jnp.where(in_head, qp, q_extra)
            kx_ref[hh, pl.ds(r0, prep), :] = jnp.where(in_head, kp, k_extra)
            ones_col = jnp.where(lane == e0, 1.0, 0.0).astype(BF16)
            vx_ref[hh, pl.ds(r0, prep), :] = jnp.where(in_head, vp, ones_col)
        return 0
    lax.fori_loop(0, seq // prep, chunk, 0)

    row = lax.broadcasted_iota(jnp.int32, (tq // 2, tq // 2), 0)
    col = lax.broadcasted_iota(jnp.int32, (tq // 2, tq // 2), 1)
    causal = col <= row
    lane = lax.broadcasted_iota(jnp.int32, (tq, LANES), 1)

    hq = tq // 2

    def qk(q0, slot, j0, width, last):
        parts = ((0, hq, width - hq), (hq, hq, width)) if last else ((0, tq, width),)
        for hh in range(2):
            for r0, nr, wd in parts:
                qx = qx_ref[hh, q0 + r0:q0 + r0 + nr, :]
                kb = kx_ref[hh, j0:j0 + wd, :]
                s_ref[slot, hh, r0:r0 + nr, 0:wd] = lax.dot_general(
                    qx, kb, (((1,), (1,)), ((), ())), preferred_element_type=F32)

    def softmax_pv(par, slot, j0, width, last):
        parts = ((0, hq, width - hq), (hq, hq, width)) if last else ((0, tq, width),)
        for hh in range(2):
            for r0, nr, wd in parts:
                s = s_ref[slot, hh, r0:r0 + nr, 0:wd]
                if last:
                    tail = jnp.where(causal, s[:, wd - hq:], NEG_INF)
                    s = tail if wd == hq else jnp.concatenate([s[:, 0:wd - hq], tail], axis=1)
                m_old = m_ref[par, hh, r0:r0 + nr, :]
                m_new = jnp.maximum(m_old, jnp.max(s, axis=-1, keepdims=True))
                alpha = jnp.exp2(m_old - m_new)
                p = jnp.exp2(s - jnp.concatenate([m_new] * (wd // LANES), axis=1)).astype(BF16)
                vb = vx_ref[hh, j0:j0 + wd, :]
                acc_ref[par, hh, r0:r0 + nr, :] = (alpha * acc_ref[par, hh, r0:r0 + nr, :]
                                                   + jnp.dot(p, vb, preferred_element_type=F32))
                m_ref[par, hh, r0:r0 + nr, :] = m_new

    nq = seq // tq
    steps = []
    for qi in range(nq):
        nkeys = (qi + 1) * tq
        j0 = 0
        while j0 < nkeys:
            width = min(2 * tq, nkeys - j0)
            steps.append((qi, j0, width, j0 + width == nkeys))
            j0 += width
    qk(0, 0, steps[0][1], steps[0][2], steps[0][3])
    for i, (qi, j0, width, last) in enumerate(steps):
        slot = i % 2
        par = qi % 2
        q0 = qi * tq
        if j0 == 0:
            m_ref[par] = jnp.full(m_ref.shape[1:], NEG_INF, F32)
            acc_ref[par] = jnp.zeros(acc_ref.shape[1:], F32)
        if i + 1 < len(steps):
            nqi, nj0, nwidth, nlast = steps[i + 1]
            qk(nqi * tq, 1 - slot, nj0, nwidth, nlast)
        softmax_pv(par, slot, j0, width, last)
        if last:
            a0 = acc_ref[par, 0]
            a1 = acc_ref[par, 1]
            o0 = a0 / a0[:, extra_lo[0]:extra_lo[0] + 1]
            o1 = a1 / a1[:, extra_lo[1]:extra_lo[1] + 1]
            o_ref[0, q0:q0 + tq, :] = jnp.where(lane < HEAD_DIM, o0, o1).astype(BF16)


def _attention(q, k, v, f, tq):
    bsz, seq, aw = q.shape
    npair = aw // LANES
    f = f.reshape(bsz, npair, f.shape[1] // npair, seq)
    prep = min(ATTN_PREP, seq)
    kern = functools.partial(_attn_kernel, tq=tq, prep=prep)
    full = lambda b, h: (b, 0, h)
    return pl.pallas_call(
        kern,
        out_shape=jax.ShapeDtypeStruct((bsz, seq, aw), BF16),
        grid=(bsz, npair),
        in_specs=[pl.BlockSpec((1, seq, LANES), full),
                  pl.BlockSpec((1, seq, LANES), full),
                  pl.BlockSpec((1, seq, LANES), full),
                  pl.BlockSpec((1, 1, f.shape[2], seq), lambda b, h: (b, h, 0, 0))],
        out_specs=pl.BlockSpec((1, seq, LANES), full),
        scratch_shapes=[pltpu.VMEM((2, seq, LANES), BF16),
                        pltpu.VMEM((2, seq, LANES), BF16),
                        pltpu.VMEM((2, seq, LANES), BF16),
                        pltpu.VMEM((2, 2, tq, LANES), F32),
                        pltpu.VMEM((2, 2, tq, LANES), F32),
                        pltpu.VMEM((2, 2, tq, 2 * tq), F32)],
        compiler_params=pltpu.CompilerParams(
            dimension_semantics=("arbitrary", "arbitrary"), vmem_limit_bytes=VMEM_LIMIT),
        name="fox_attention",
    )(q, k, v, f)


def _ssm_param_kernel(are_ref, aim_ref, ldt_ref, bre_ref, bim_ref,
                      lre_ref, lim_ref, bbre_ref, bbim_ref):
    a_re = are_ref[...]
    a_im = aim_ref[...]
    dt = jnp.exp(ldt_ref[...])
    mag = jnp.exp(a_re * dt)
    ang = a_im * dt
    l_re = mag * jnp.cos(ang)
    l_im = mag * jnp.sin(ang)
    lre_ref[...] = l_re
    lim_ref[...] = l_im
    n_re = l_re - 1.0
    den = a_re * a_re + a_im * a_im
    c_re = (n_re * a_re + l_im * a_im) / den
    c_im = (l_im * a_re - n_re * a_im) / den
    c_re = c_re[:, None, :]
    c_im = c_im[:, None, :]
    b_re = bre_ref[...]
    b_im = bim_ref[...]
    bbre_ref[...] = c_re * b_re - c_im * b_im
    bbim_ref[...] = c_re * b_im + c_im * b_re


def _ssm_params(a_re, a_im, log_dt, b_re_t, b_im_t):
    g, p = a_re.shape
    c = b_re_t.shape[1]
    return pl.pallas_call(
        _ssm_param_kernel,
        out_shape=(jax.ShapeDtypeStruct((g, p), F32), jax.ShapeDtypeStruct((g, p), F32),
                   jax.ShapeDtypeStruct((g, c, p), F32), jax.ShapeDtypeStruct((g, c, p), F32)),
        name="ssm_params",
    )(a_re, a_im, log_dt.reshape(g, 1), b_re_t, b_im_t)


def _ssm_kernel(u_ref, perm_ref, permt_ref, lre_ref, lim_ref, bmat_ref, cre_ref, cim_ref, dskip_ref,
                wglu_ref, bglu_ref, gout_ref, o_ref, bu_ref, h_ref, sre_ref, sim_ref, *, nb, tt, th):
    nchunk = bmat_ref.shape[0]
    cw = bmat_ref.shape[2] // 2
    iw = bmat_ref.shape[1]
    w = u_ref.shape[2]

    @pl.when(pl.program_id(0) == 0)
    def _():
        sre_ref[...] = jnp.zeros_like(sre_ref)
        sim_ref[...] = jnp.zeros_like(sim_ref)

    parts = []
    for g in range(tt // th):
        ug = u_ref[:, g * th:(g + 1) * th, :].reshape(nb * th, w)
        parts.append(jnp.dot(perm_ref[...], ug, preferred_element_type=F32).astype(BF16))
    u = jnp.concatenate(parts, axis=0)

    def bu(c):
        base = 2 * cw * c
        bu_ref[:, base:base + 2 * cw] = jnp.dot(u[:, iw * c:iw * (c + 1)], bmat_ref[c],
                                                preferred_element_type=F32)

    def scan(c):
        base = 2 * cw * c
        a_re = jnp.broadcast_to(lre_ref[c:c + 1, :], (nb, cw))
        a_im = jnp.broadcast_to(lim_ref[c:c + 1, :], (nb, cw))
        h_re = sre_ref[:, cw * c:cw * (c + 1)]
        h_im = sim_ref[:, cw * c:cw * (c + 1)]
        for t in range(tt):
            r0 = t * nb
            b_re = bu_ref[r0:r0 + nb, base:base + cw]
            b_im = bu_ref[r0:r0 + nb, base + cw:base + 2 * cw]
            h_re, h_im = (a_re * h_re - a_im * h_im + b_re, a_re * h_im + a_im * h_re + b_im)
            h_ref[r0:r0 + nb, base:base + cw] = h_re.astype(BF16)
            h_ref[r0:r0 + nb, base + cw:base + 2 * cw] = h_im.astype(BF16)
        sre_ref[:, cw * c:cw * (c + 1)] = h_re
        sim_ref[:, cw * c:cw * (c + 1)] = h_im

    def readout(c):
        base = 2 * cw * c
        y = jnp.dot(h_ref[:, base:base + cw], cre_ref[c], preferred_element_type=F32)
        return y - jnp.dot(h_ref[:, base + cw:base + 2 * cw], cim_ref[c], preferred_element_type=F32)

    ys = []
    bu(0)
    for c in range(nchunk):
        if c + 1 < nchunk:
            bu(c + 1)
        scan(c)
        ys.append(readout(c))
    y = jnp.concatenate(ys, axis=-1) + dskip_ref[...] * u.astype(F32)
    z = jax.nn.gelu(y, approximate=True)
    gate = jnp.dot(z.astype(BF16), wglu_ref[...], preferred_element_type=F32) + bglu_ref[...]
    out = z * jax.nn.sigmoid(gate)
    out = (out * _rms_scale(out) * gout_ref[...]).astype(BF16)
    for g in range(tt // th):
        og = jnp.dot(permt_ref[...], out[g * th * nb:(g + 1) * th * nb], preferred_element_type=F32)
        o_ref[:, g * th:(g + 1) * th, :] = og.astype(BF16).reshape(nb, th, w)


def _ssm(u, l_re, l_im, bmat, cre, cim, dskip, wglu_bd, bglu, g_out, tt):
    nb, seq, w = u.shape
    th = SUBLANES_BF16
    rows = nb * tt
    nchunk, iw, cw2 = bmat.shape
    cw = cw2 // 2
    src = (jnp.arange(nb)[None, :] * th + jnp.arange(th)[:, None]).reshape(-1)
    perm = (src[:, None] == jnp.arange(nb * th)[None, :]).astype(BF16)
    const2 = lambda i: (0, 0)
    const3 = lambda i: (0, 0, 0)
    kern = functools.partial(_ssm_kernel, nb=nb, tt=tt, th=th)
    return pl.pallas_call(
        kern,
        out_shape=jax.ShapeDtypeStruct((nb, seq, w), BF16),
        grid=(seq // tt,),
        in_specs=[pl.BlockSpec((nb, tt, w), lambda i: (0, i, 0)),
                  pl.BlockSpec((nb * th, nb * th), const2),
                  pl.BlockSpec((nb * th, nb * th), const2),
                  pl.BlockSpec((nchunk, cw), const2),
                  pl.BlockSpec((nchunk, cw), const2),
                  pl.BlockSpec((nchunk, iw, cw2), const3),
                  pl.BlockSpec((nchunk, cw, iw), const3),
                  pl.BlockSpec((nchunk, cw, iw), const3),
                  pl.BlockSpec((1, w), const2),
                  pl.BlockSpec((w, w), const2),
                  pl.BlockSpec((1, w), const2),
                  pl.BlockSpec((1, w), const2)],
        out_specs=pl.BlockSpec((nb, tt, w), lambda i: (0, i, 0)),
        scratch_shapes=[pltpu.VMEM((rows, nchunk * cw2), F32),
                        pltpu.VMEM((rows, nchunk * cw2), BF16),
                        pltpu.VMEM((nb, nchunk * cw), F32),
                        pltpu.VMEM((nb, nchunk * cw), F32)],
        compiler_params=pltpu.CompilerParams(
            dimension_semantics=("arbitrary",), vmem_limit_bytes=VMEM_LIMIT),
        name="s5_ssm",
    )(u, perm, perm.T, l_re, l_im, bmat, cre, cim, dskip, wglu_bd, bglu, g_out)


def _outffn_kernel(x_ref, attn_ref, ssm_ref, mod_ref, gattn_ref, wout_ref, gffn_ref, wup_ref,
                   convw_ref, convb_ref, wdown_ref, gfin_ref, o_ref, carry_ref, *, d, dff):
    si = pl.program_id(1)
    tm = x_ref.shape[1]

    @pl.when(si == 0)
    def _():
        carry_ref[...] = jnp.zeros_like(carry_ref)

    gt_m = mod_ref[0, :, 2 * d:3 * d]
    sh_f = mod_ref[0, :, 3 * d:4 * d]
    sc_f = mod_ref[0, :, 4 * d:5 * d]
    gt_f = mod_ref[0, :, 5 * d:6 * d]

    a = attn_ref[0].astype(F32)
    an = (a * _rms_scale(a) * gattn_ref[...]).astype(BF16)
    mixin = jnp.concatenate([an, ssm_ref[0]], axis=-1)
    mix = jnp.dot(mixin, wout_ref[...], preferred_element_type=F32)
    x1 = x_ref[0] + gt_m * mix

    h = x1 * _rms_scale(x1) * gffn_ref[...]
    h = (h * (1.0 + sc_f) + sh_f).astype(BF16)
    up = jnp.dot(h, wup_ref[...], preferred_element_type=F32)
    gp = up[:, 0:dff]
    val = up[:, dff:2 * dff]

    prev = carry_ref[...]
    r8 = lax.broadcasted_iota(jnp.int32, prev.shape, 0)

    def shifted(k):
        body = pltpu.roll(gp, k, 0)
        head = jnp.where(r8 < k, pltpu.roll(prev, k, 0), body[0:8])
        return jnp.concatenate([head, body[8:]], axis=0)

    conv = (convw_ref[0:1, :] * shifted(2) + convw_ref[1:2, :] * shifted(1)
            + convw_ref[2:3, :] * gp + convb_ref[...])
    carry_ref[...] = gp[tm - 8:tm]
    act = (conv * jax.nn.sigmoid(conv) * val).astype(BF16)
    y = jnp.dot(act, wdown_ref[...], preferred_element_type=F32)
    x2 = x1 + gt_f * y
    o_ref[0] = x2 * _rms_scale(x2) * gfin_ref[...]


def _out_ffn(x, attn, ssm, mod3, g_attn, w_out, g_ffn, w_up, conv_w, conv_b, w_down, g_final, tm):
    bsz, seq, d = x.shape
    aw = attn.shape[2]
    sw = ssm.shape[2]
    dff = w_down.shape[0]
    const = lambda b, s: (0, 0)
    rows = lambda b, s: (b, s, 0)
    single = pl.Buffered(1)
    kern = functools.partial(_outffn_kernel, d=d, dff=dff)
    return pl.pallas_call(
        kern,
        out_shape=jax.ShapeDtypeStruct((bsz, seq, d), F32),
        grid=(bsz, seq // tm),
        in_specs=[pl.BlockSpec((1, tm, d), rows),
                  pl.BlockSpec((1, tm, aw), rows),
                  pl.BlockSpec((1, tm, sw), rows),
                  pl.BlockSpec((1, 1, mod3.shape[2]), lambda b, s: (b, 0, 0)),
                  pl.BlockSpec((1, aw), const),
                  pl.BlockSpec((aw + sw, d), const, pipeline_mode=single),
                  pl.BlockSpec((1, d), const),
                  pl.BlockSpec((d, 2 * dff), const, pipeline_mode=single),
                  pl.BlockSpec((CONV_WIDTH, dff), const),
                  pl.BlockSpec((1, dff), const),
                  pl.BlockSpec((dff, d), const, pipeline_mode=single),
                  pl.BlockSpec((1, d), const)],
        out_specs=pl.BlockSpec((1, tm, d), rows),
        scratch_shapes=[pltpu.VMEM((8, dff), F32)],
        compiler_params=pltpu.CompilerParams(
            dimension_semantics=("arbitrary", "arbitrary"), vmem_limit_bytes=VMEM_LIMIT),
        name="out_ffn",
    )(x, attn, ssm, mod3, g_attn, w_out, g_ffn, w_up, conv_w, conv_b, w_down, g_final)


def _block_diag(blocks):
    n, r, c = blocks.shape
    eye = jnp.eye(n, dtype=blocks.dtype)
    return (blocks[:, :, None, :] * eye[:, None, :, None]).reshape(n * r, n * c)


def _chunked_block_diag(blocks, per):
    g, r, c = blocks.shape
    return jax.vmap(_block_diag)(blocks.reshape(g // per, per, r, c))


def kernel(x, c, w_ada, b_ada, g_mix, w_in, b_fgate, a_re, a_im, log_dt, ssm_b_re, ssm_b_im,
           ssm_c_re, ssm_c_im, d_skip, w_glu, b_glu, g_attn_out, g_ssm_out, w_out, g_ffn, w_up,
           conv_w, conv_b, w_down, g_final):
    bsz, seq, d = x.shape
    depth = w_ada.shape[0]
    aw = N_HEADS * HEAD_DIM
    ngroups = a_re.shape[1]
    sw = ngroups * SSM_GROUP
    per = GROUPS_PER_CHUNK
    tm_in = min(TM_IN, seq)
    tq = min(TQ, seq)
    tm_ffn = min(TM_FFN, seq)
    tt = min(TT, seq)
    assert bsz % 8 == 0 and seq % tm_in == 0 and seq % tq == 0 and ngroups % per == 0
    assert tq % (2 * LANES) == 0
    assert seq % tt == 0 and tt % SUBLANES_BF16 == 0

    for l in range(depth):
        mod3 = _modulation(c, w_ada[l], b_ada[l]).reshape(bsz, 1, N_MOD * d)

        w = w_in[l]
        w_cat = jnp.concatenate([w[:, 0:3 * aw], w[:, 3 * aw + N_HEADS:]], axis=1).astype(BF16)
        wf_t = w[:, 3 * aw:3 * aw + N_HEADS].T.astype(BF16)
        q, k, v, f, u = _in_proj(x, mod3, g_mix[l].reshape(1, d), w_cat, wf_t,
                                 b_fgate[l].reshape(N_HEADS, 1), tm_in)

        attn = _attention(q, k, v, f, tq)

        l_re, l_im, bb_re, bb_im = _ssm_params(
            a_re[l], a_im[l], log_dt[l],
            ssm_b_re[l].transpose(0, 2, 1), ssm_b_im[l].transpose(0, 2, 1))
        bmat = jnp.concatenate([_chunked_block_diag(bb_re, per), _chunked_block_diag(bb_im, per)],
                               axis=-1).astype(BF16)
        cre = _chunked_block_diag(ssm_c_re[l].transpose(0, 2, 1), per).astype(BF16)
        cim = _chunked_block_diag(ssm_c_im[l].transpose(0, 2, 1), per).astype(BF16)
        wglu_bd = _block_diag(w_glu[l]).astype(BF16)
        ssm = _ssm(u, l_re.reshape(ngroups // per, per * STATE_DIM),
                   l_im.reshape(ngroups // per, per * STATE_DIM),
                   bmat, cre, cim, d_skip[l].reshape(1, sw), wglu_bd, b_glu[l].reshape(1, sw),
                   g_ssm_out[l].reshape(1, sw), tt)

        assert depth == 1
        x = _out_ffn(x, attn, ssm, mod3, g_attn_out[l].reshape(1, aw), w_out[l].astype(BF16),
                     g_ffn[l].reshape(1, d), w_up[l].astype(BF16), conv_w[l], conv_b[l].reshape(1, -1),
                     w_down[l].astype(BF16), g_final.reshape(1, d), tm_ffn)
    return x
```

```python
import functools
import math

import jax
import jax.numpy as jnp
from jax import lax
from jax.experimental import pallas as pl
from jax.experimental.pallas import tpu as pltpu

EPS = 1e-6
NEG_INF = -1e30
HEAD_DIM = 64
N_HEADS = 8
SSM_GROUP = 16
STATE_DIM = 64
CONV_WIDTH = 3
N_MOD = 6

LANES = 128
SUBLANES = 8
SUBLANES_BF16 = 16
GROUPS_PER_CHUNK = 8
VMEM_LIMIT = 56 * 1024 * 1024
TM_IN = 1024
TQ = 512
ATTN_PREP = 512
TM_FFN = 512
TT = 64
SSM_PIECE_STEPS = 8
LOG2E = math.log2(math.e)

BF16 = jnp.bfloat16
F32 = jnp.float32


def _split3(a):
    hi = a.astype(BF16)
    r1 = a - hi.astype(F32)
    mid = r1.astype(BF16)
    lo = (r1 - mid.astype(F32)).astype(BF16)
    return hi, mid, lo


def _rms_scale(x):
    return lax.rsqrt(jnp.mean(x * x, axis=-1, keepdims=True) + EPS)


def _mod_kernel(c_ref, w_ref, b_ref, o_ref):
    c = c_ref[...]
    s = c * jax.nn.sigmoid(c)
    w = w_ref[...]
    s_hi = s.astype(BF16)
    s_lo = (s - s_hi.astype(F32)).astype(BF16)
    w_hi = w.astype(BF16)
    w_lo = (w - w_hi.astype(F32)).astype(BF16)
    acc = jnp.dot(s_hi, w_hi, preferred_element_type=F32)
    acc += jnp.dot(s_lo, w_hi, preferred_element_type=F32)
    acc += jnp.dot(s_hi, w_lo, preferred_element_type=F32)
    o_ref[...] = acc + b_ref[...]


def _modulation(c, w_ada, b_ada):
    bsz, d = c.shape
    n = w_ada.shape[1]
    tn = 512
    return pl.pallas_call(
        _mod_kernel,
        out_shape=jax.ShapeDtypeStruct((bsz, n), F32),
        grid=(n // tn,),
        in_specs=[pl.BlockSpec((bsz, d), lambda j: (0, 0)),
                  pl.BlockSpec((d, tn), lambda j: (0, j)),
                  pl.BlockSpec((1, tn), lambda j: (0, j))],
        out_specs=pl.BlockSpec((bsz, tn), lambda j: (0, j)),
        compiler_params=pltpu.CompilerParams(dimension_semantics=("arbitrary",)),
        name="modulation",
    )(c, w_ada, b_ada.reshape(1, n))


def _inproj_kernel(x_ref, mod_ref, g_ref, w_ref, wft_ref, bf_ref, tri_ref,
                   q_ref, k_ref, v_ref, f_ref, u_ref, carry_ref, *, d, aw):
    si = pl.program_id(1)
    tm = x_ref.shape[1]

    @pl.when(si == 0)
    def _():
        carry_ref[...] = jnp.zeros_like(carry_ref)

    x = x_ref[0]
    shift = mod_ref[0, :, 0:d]
    scale = mod_ref[0, :, d:2 * d]
    h = x * _rms_scale(x) * g_ref[...]
    h = h * (1.0 + scale) + shift
    hb = h.astype(BF16)

    proj = jnp.dot(hb, w_ref[...], preferred_element_type=F32)
    q_ref[0] = (proj[:, 0:aw] * (HEAD_DIM ** -0.5 * LOG2E)).astype(BF16)
    k_ref[0] = proj[:, aw:2 * aw].astype(BF16)
    v_ref[0] = proj[:, 2 * aw:3 * aw].astype(BF16)
    u_ref[0] = proj[:, 3 * aw:].astype(BF16)

    z = lax.dot_general(wft_ref[...], hb, (((1,), (1,)), ((), ())),
                        preferred_element_type=F32) + bf_ref[...]
    logf = jnp.minimum(z, 0.0) - jnp.log1p(jnp.exp(-jnp.abs(z)))
    pieces = jnp.concatenate(_split3(logf), axis=0)
    cs = jnp.dot(pieces, tri_ref[...], preferred_element_type=F32)
    nh = logf.shape[0]
    cum = cs[0:nh] + cs[nh:2 * nh] + cs[2 * nh:3 * nh]
    cum = cum + carry_ref[:, LANES - 1:LANES]
    f_ref[0] = cum * LOG2E
    carry_ref[...] = cum[:, tm - LANES:tm]


def _in_proj(x, mod3, g_mix, w_cat, wf_t, b_f, tm):
    bsz, seq, d = x.shape
    ncat = w_cat.shape[1]
    aw = N_HEADS * HEAD_DIM
    sw = ncat - 3 * aw
    nh = wf_t.shape[0]
    tri = (jnp.arange(tm)[:, None] <= jnp.arange(tm)[None, :]).astype(BF16)
    const = lambda b, s: (0, 0)
    rows = lambda b, s: (b, s, 0)
    kern = functools.partial(_inproj_kernel, d=d, aw=aw)
    return pl.pallas_call(
        kern,
        out_shape=(jax.ShapeDtypeStruct((bsz, seq, aw), BF16),
                   jax.ShapeDtypeStruct((bsz, seq, aw), BF16),
                   jax.ShapeDtypeStruct((bsz, seq, aw), BF16),
                   jax.ShapeDtypeStruct((bsz, nh, seq), F32),
                   jax.ShapeDtypeStruct((bsz, seq, sw), BF16)),
        grid=(bsz, seq // tm),
        in_specs=[pl.BlockSpec((1, tm, d), rows),
                  pl.BlockSpec((1, 1, mod3.shape[2]), lambda b, s: (b, 0, 0)),
                  pl.BlockSpec((1, d), const),
                  pl.BlockSpec((d, ncat), const, pipeline_mode=pl.Buffered(1)),
                  pl.BlockSpec((nh, d), const),
                  pl.BlockSpec((nh, 1), const),
                  pl.BlockSpec((tm, tm), const, pipeline_mode=pl.Buffered(1))],
        out_specs=(pl.BlockSpec((1, tm, aw), rows),
                   pl.BlockSpec((1, tm, aw), rows),
                   pl.BlockSpec((1, tm, aw), rows),
                   pl.BlockSpec((1, nh, tm), lambda b, s: (b, 0, s)),
                   pl.BlockSpec((1, tm, sw), rows)),
        scratch_shapes=[pltpu.VMEM((nh, LANES), F32)],
        compiler_params=pltpu.CompilerParams(
            dimension_semantics=("arbitrary", "arbitrary"), vmem_limit_bytes=VMEM_LIMIT),
        name="in_proj",
    )(x, mod3, g_mix, w_cat, wf_t, b_f, tri)


def _attn_kernel(q_ref, k_ref, v_ref, f_ref, o_ref, qx_ref, kx_ref, vx_ref, m_ref, acc_ref, s_ref,
                 *, tq, prep):
    seq = k_ref.shape[1]
    data_lo = (0, HEAD_DIM)
    extra_lo = (HEAD_DIM, 0)

    def chunk(ci, _):
        r0 = pl.multiple_of(ci * prep, prep)
        lane = lax.broadcasted_iota(jnp.int32, (prep, LANES), 1)
        sub = lax.broadcasted_iota(jnp.int32, (SUBLANES, prep), 0)
        qp = q_ref[0, pl.ds(r0, prep), :]
        kp = k_ref[0, pl.ds(r0, prep), :]
        vp = v_ref[0, pl.ds(r0, prep), :]
        for hh in range(2):
            e0 = extra_lo[hh]
            hi, mid, lo = [x.astype(F32) for x in _split3(f_ref[0, 0, hh:hh + 1, pl.ds(r0, prep)])]
            one = jnp.ones_like(hi)

            def columns(vals, e0=e0):
                tile = jnp.zeros((SUBLANES, prep), F32)
                for i, val in enumerate(vals):
                    tile = jnp.where(sub == i, val, tile)
                rows = [jnp.zeros((e0, prep), F32)] if e0 else []
                rows += [tile, jnp.zeros((LANES - e0 - SUBLANES, prep), F32)]
                return jnp.concatenate(rows, axis=0).T.astype(BF16)

            q_extra = columns([hi, mid, lo, one, one, one])
            k_extra = columns([one, one, one, -hi, -mid, -lo])
            in_head = (lane >= data_lo[hh]) & (lane < data_lo[hh] + HEAD_DIM)
            keep = jnp.where(in_head, 1.0, 0.0).astype(BF16)
            qx_ref[hh, pl.ds(r0, prep), :] = qp * keep + q_extra
            kx_ref[hh, pl.ds(r0, prep), :] = kp * keep + k_extra
            ones_col = jnp.where(lane == e0, 1.0, 0.0).astype(BF16)
            vx_ref[hh, pl.ds(r0, prep), :] = vp * keep + ones_col
        return 0
    lax.fori_loop(0, seq // prep, chunk, 0)

    row = lax.broadcasted_iota(jnp.int32, (tq // 2, tq // 2), 0)
    col = lax.broadcasted_iota(jnp.int32, (tq // 2, tq // 2), 1)
    causal = col <= row
    lane = lax.broadcasted_iota(jnp.int32, (tq, LANES), 1)

    hq = tq // 2

    def qk(q0, slot, j0, width, last):
        parts = ((0, hq, width - hq), (hq, hq, width)) if last else ((0, tq, width),)
        for hh in range(2):
            for r0, nr, wd in parts:
                qx = qx_ref[hh, q0 + r0:q0 + r0 + nr, :]
                kb = kx_ref[hh, j0:j0 + wd, :]
                s_ref[slot, hh, r0:r0 + nr, 0:wd] = lax.dot_general(
                    qx, kb, (((1,), (1,)), ((), ())), preferred_element_type=F32)

    def softmax_pv(par, slot, j0, width, last):
        parts = ((0, hq, width - hq), (hq, hq, width)) if last else ((0, tq, width),)
        for hh in range(2):
            for r0, nr, wd in parts:
                s = s_ref[slot, hh, r0:r0 + nr, 0:wd]
                if last:
                    tail = jnp.where(causal, s[:, wd - hq:], NEG_INF)
                    s = tail if wd == hq else jnp.concatenate([s[:, 0:wd - hq], tail], axis=1)
                m_old = m_ref[par, hh, r0:r0 + nr, :]
                m_new = jnp.maximum(m_old, jnp.max(s, axis=-1, keepdims=True))
                alpha = jnp.exp2(m_old - m_new)
                p = jnp.exp2(s - jnp.concatenate([m_new] * (wd // LANES), axis=1)).astype(BF16)
                vb = vx_ref[hh, j0:j0 + wd, :]
                acc_ref[par, hh, r0:r0 + nr, :] = (alpha * acc_ref[par, hh, r0:r0 + nr, :]
                                                   + jnp.dot(p, vb, preferred_element_type=F32))
                m_ref[par, hh, r0:r0 + nr, :] = m_new

    nq = seq // tq
    steps = []
    for qi in range(nq):
        nkeys = (qi + 1) * tq
        j0 = 0
        while j0 < nkeys:
            width = min(2 * tq, nkeys - j0)
            steps.append((qi, j0, width, j0 + width == nkeys))
            j0 += width
    qk(0, 0, steps[0][1], steps[0][2], steps[0][3])
    for i, (qi, j0, width, last) in enumerate(steps):
        slot = i % 2
        par = qi % 2
        q0 = qi * tq
        if j0 == 0:
            m_ref[par] = jnp.full(m_ref.shape[1:], NEG_INF, F32)
            acc_ref[par] = jnp.zeros(acc_ref.shape[1:], F32)
        if i + 1 < len(steps):
            nqi, nj0, nwidth, nlast = steps[i + 1]
            qk(nqi * tq, 1 - slot, nj0, nwidth, nlast)
        softmax_pv(par, slot, j0, width, last)
        if last:
            a0 = acc_ref[par, 0]
            a1 = acc_ref[par, 1]
            o0 = a0 / a0[:, extra_lo[0]:extra_lo[0] + 1]
            o1 = a1 / a1[:, extra_lo[1]:extra_lo[1] + 1]
            o_ref[0, q0:q0 + tq, :] = jnp.where(lane < HEAD_DIM, o0, o1).astype(BF16)


def _attention(q, k, v, f, tq):
    bsz, seq, aw = q.shape
    npair = aw // LANES
    f = f.reshape(bsz, npair, f.shape[1] // npair, seq)
    prep = min(ATTN_PREP, seq)
    kern = functools.partial(_attn_kernel, tq=tq, prep=prep)
    full = lambda b, h: (b, 0, h)
    return pl.pallas_call(
        kern,
        out_shape=jax.ShapeDtypeStruct((bsz, seq, aw), BF16),
        grid=(bsz, npair),
        in_specs=[pl.BlockSpec((1, seq, LANES), full),
                  pl.BlockSpec((1, seq, LANES), full),
                  pl.BlockSpec((1, seq, LANES), full),
                  pl.BlockSpec((1, 1, f.shape[2], seq), lambda b, h: (b, h, 0, 0))],
        out_specs=pl.BlockSpec((1, seq, LANES), full),
        scratch_shapes=[pltpu.VMEM((2, seq, LANES), BF16),
                        pltpu.VMEM((2, seq, LANES), BF16),
                        pltpu.VMEM((2, seq, LANES), BF16),
                        pltpu.VMEM((2, 2, tq, LANES), F32),
                        pltpu.VMEM((2, 2, tq, LANES), F32),
                        pltpu.VMEM((2, 2, tq, 2 * tq), F32)],
        compiler_params=pltpu.CompilerParams(
            dimension_semantics=("arbitrary", "arbitrary"), vmem_limit_bytes=VMEM_LIMIT),
        name="fox_attention",
    )(q, k, v, f)


def _ssm_param_kernel(are_ref, aim_ref, ldt_ref, bre_ref, bim_ref,
                      lre_ref, lim_ref, bbre_ref, bbim_ref):
    a_re = are_ref[...]
    a_im = aim_ref[...]
    dt = jnp.exp(ldt_ref[...])
    mag = jnp.exp(a_re * dt)
    ang = a_im * dt
    l_re = mag * jnp.cos(ang)
    l_im = mag * jnp.sin(ang)
    lre_ref[...] = l_re
    lim_ref[...] = l_im
    n_re = l_re - 1.0
    den = a_re * a_re + a_im * a_im
    c_re = (n_re * a_re + l_im * a_im) / den
    c_im = (l_im * a_re - n_re * a_im) / den
    c_re = c_re[:, None, :]
    c_im = c_im[:, None, :]
    b_re = bre_ref[...]
    b_im = bim_ref[...]
    bbre_ref[...] = c_re * b_re - c_im * b_im
    bbim_ref[...] = c_re * b_im + c_im * b_re


def _ssm_params(a_re, a_im, log_dt, b_re_t, b_im_t):
    g, p = a_re.shape
    c = b_re_t.shape[1]
    return pl.pallas_call(
        _ssm_param_kernel,
        out_shape=(jax.ShapeDtypeStruct((g, p), F32), jax.ShapeDtypeStruct((g, p), F32),
                   jax.ShapeDtypeStruct((g, c, p), F32), jax.ShapeDtypeStruct((g, c, p), F32)),
        name="ssm_params",
    )(a_re, a_im, log_dt.reshape(g, 1), b_re_t, b_im_t)


def _ssm_kernel(u_ref, perm_ref, permt_ref, lre_ref, lim_ref, bmat_ref, cre_ref, cim_ref, dskip_ref,
                wglu_ref, bglu_ref, gout_ref, o_ref, bu_ref, h_ref, sre_ref, sim_ref, *, nb, tt, th):
    nchunk = bmat_ref.shape[0]
    cw = bmat_ref.shape[2] // 2
    iw = bmat_ref.shape[1]
    w = u_ref.shape[2]

    @pl.when(pl.program_id(0) == 0)
    def _():
        sre_ref[...] = jnp.zeros_like(sre_ref)
        sim_ref[...] = jnp.zeros_like(sim_ref)

    parts = []
    for g in range(tt // th):
        ug = u_ref[:, g * th:(g + 1) * th, :].reshape(nb * th, w)
        parts.append(jnp.dot(perm_ref[...], ug, preferred_element_type=F32).astype(BF16))
    u = jnp.concatenate(parts, axis=0)

    spb = SSM_PIECE_STEPS
    rb = spb * nb
    npiece = tt // spb

    def bu(c, k):
        base = 2 * cw * c
        bu_ref[k * rb:(k + 1) * rb, base:base + 2 * cw] = jnp.dot(
            u[k * rb:(k + 1) * rb, iw * c:iw * (c + 1)], bmat_ref[c], preferred_element_type=F32)

    def readout(c, k):
        base = 2 * cw * c
        rows = slice(k * rb, (k + 1) * rb)
        y = jnp.dot(h_ref[rows, base:base + cw], cre_ref[c], preferred_element_type=F32)
        return y - jnp.dot(h_ref[rows, base + cw:base + 2 * cw], cim_ref[c], preferred_element_type=F32)

    ys = [[None] * npiece for _ in range(nchunk)]
    for k in range(npiece):
        bu(0, k)
    for c in range(nchunk + 1):
        if c < nchunk:
            base = 2 * cw * c
            a_re = jnp.broadcast_to(lre_ref[c:c + 1, :], (nb, cw))
            a_im = jnp.broadcast_to(lim_ref[c:c + 1, :], (nb, cw))
            h_re = sre_ref[:, cw * c:cw * (c + 1)]
            h_im = sim_ref[:, cw * c:cw * (c + 1)]
        for k in range(npiece):
            if c + 1 < nchunk:
                bu(c + 1, k)
            if c >= 1:
                ys[c - 1][k] = readout(c - 1, k)
            if c < nchunk:
                for t in range(k * spb, (k + 1) * spb):
                    r0 = t * nb
                    b_re = bu_ref[r0:r0 + nb, base:base + cw]
                    b_im = bu_ref[r0:r0 + nb, base + cw:base + 2 * cw]
                    h_re, h_im = (a_re * h_re - a_im * h_im + b_re, a_re * h_im + a_im * h_re + b_im)
                    h_ref[r0:r0 + nb, base:base + cw] = h_re.astype(BF16)
                    h_ref[r0:r0 + nb, base + cw:base + 2 * cw] = h_im.astype(BF16)
        if c < nchunk:
            sre_ref[:, cw * c:cw * (c + 1)] = h_re
            sim_ref[:, cw * c:cw * (c + 1)] = h_im
    ys = [jnp.concatenate(pieces, axis=0) for pieces in ys]
    y = jnp.concatenate(ys, axis=-1) + dskip_ref[...] * u.astype(F32)
    z = jax.nn.gelu(y, approximate=True)
    gate = jnp.dot(z.astype(BF16), wglu_ref[...], preferred_element_type=F32) + bglu_ref[...]
    out = z * jax.nn.sigmoid(gate)
    out = (out * _rms_scale(out) * gout_ref[...]).astype(BF16)
    for g in range(tt // th):
        og = jnp.dot(permt_ref[...], out[g * th * nb:(g + 1) * th * nb], preferred_element_type=F32)
        o_ref[:, g * th:(g + 1) * th, :] = og.astype(BF16).reshape(nb, th, w)


def _ssm(u, l_re, l_im, bmat, cre, cim, dskip, wglu_bd, bglu, g_out, tt):
    nb, seq, w = u.shape
    th = SUBLANES_BF16
    rows = nb * tt
    nchunk, iw, cw2 = bmat.shape
    cw = cw2 // 2
    src = (jnp.arange(nb)[None, :] * th + jnp.arange(th)[:, None]).reshape(-1)
    perm = (src[:, None] == jnp.arange(nb * th)[None, :]).astype(BF16)
    const2 = lambda i: (0, 0)
    const3 = lambda i: (0, 0, 0)
    kern = functools.partial(_ssm_kernel, nb=nb, tt=tt, th=th)
    return pl.pallas_call(
        kern,
        out_shape=jax.ShapeDtypeStruct((nb, seq, w), BF16),
        grid=(seq // tt,),
        in_specs=[pl.BlockSpec((nb, tt, w), lambda i: (0, i, 0)),
                  pl.BlockSpec((nb * th, nb * th), const2),
                  pl.BlockSpec((nb * th, nb * th), const2),
                  pl.BlockSpec((nchunk, cw), const2),
                  pl.BlockSpec((nchunk, cw), const2),
                  pl.BlockSpec((nchunk, iw, cw2), const3),
                  pl.BlockSpec((nchunk, cw, iw), const3),
                  pl.BlockSpec((nchunk, cw, iw), const3),
                  pl.BlockSpec((1, w), const2),
                  pl.BlockSpec((w, w), const2),
                  pl.BlockSpec((1, w), const2),
                  pl.BlockSpec((1, w), const2)],
        out_specs=pl.BlockSpec((nb, tt, w), lambda i: (0, i, 0)),
        scratch_shapes=[pltpu.VMEM((rows, nchunk * cw2), F32),
                        pltpu.VMEM((rows, nchunk * cw2), BF16),
                        pltpu.VMEM((nb, nchunk * cw), F32),
                        pltpu.VMEM((nb, nchunk * cw), F32)],
        compiler_params=pltpu.CompilerParams(
            dimension_semantics=("arbitrary",), vmem_limit_bytes=VMEM_LIMIT),
        name="s5_ssm",
    )(u, perm, perm.T, l_re, l_im, bmat, cre, cim, dskip, wglu_bd, bglu, g_out)


def _outffn_kernel(x_ref, attn_ref, ssm_ref, mod_ref, gattn_ref, wout_ref, gffn_ref, wup_ref,
                   convw_ref, convb_ref, wdown_ref, gfin_ref, o_ref, carry_ref, *, d, dff):
    si = pl.program_id(1)
    tm = x_ref.shape[1]

    @pl.when(si == 0)
    def _():
        carry_ref[...] = jnp.zeros_like(carry_ref)

    gt_m = mod_ref[0, :, 2 * d:3 * d]
    sh_f = mod_ref[0, :, 3 * d:4 * d]
    sc_f = mod_ref[0, :, 4 * d:5 * d]
    gt_f = mod_ref[0, :, 5 * d:6 * d]

    a = attn_ref[0].astype(F32)
    an = (a * _rms_scale(a) * gattn_ref[...]).astype(BF16)
    mixin = jnp.concatenate([an, ssm_ref[0]], axis=-1)
    mix = jnp.dot(mixin, wout_ref[...], preferred_element_type=F32)
    x1 = x_ref[0] + gt_m * mix

    h = x1 * _rms_scale(x1) * gffn_ref[...]
    h = (h * (1.0 + sc_f) + sh_f).astype(BF16)
    up = jnp.dot(h, wup_ref[...], preferred_element_type=F32)
    gp = up[:, 0:dff]
    val = up[:, dff:2 * dff]

    prev = carry_ref[...]
    r8 = lax.broadcasted_iota(jnp.int32, prev.shape, 0)

    def shifted(k):
        body = pltpu.roll(gp, k, 0)
        head = jnp.where(r8 < k, pltpu.roll(prev, k, 0), body[0:8])
        return jnp.concatenate([head, body[8:]], axis=0)

    conv = (convw_ref[0:1, :] * shifted(2) + convw_ref[1:2, :] * shifted(1)
            + convw_ref[2:3, :] * gp + convb_ref[...])
    carry_ref[...] = gp[tm - 8:tm]
    act = (conv * jax.nn.sigmoid(conv) * val).astype(BF16)
    y = jnp.dot(act, wdown_ref[...], preferred_element_type=F32)
    x2 = x1 + gt_f * y
    o_ref[0] = x2 * _rms_scale(x2) * gfin_ref[...]


def _out_ffn(x, attn, ssm, mod3, g_attn, w_out, g_ffn, w_up, conv_w, conv_b, w_down, g_final, tm):
    bsz, seq, d = x.shape
    aw = attn.shape[2]
    sw = ssm.shape[2]
    dff = w_down.shape[0]
    const = lambda b, s: (0, 0)
    rows = lambda b, s: (b, s, 0)
    single = pl.Buffered(1)
    kern = functools.partial(_outffn_kernel, d=d, dff=dff)
    return pl.pallas_call(
        kern,
        out_shape=jax.ShapeDtypeStruct((bsz, seq, d), F32),
        grid=(bsz, seq // tm),
        in_specs=[pl.BlockSpec((1, tm, d), rows),
                  pl.BlockSpec((1, tm, aw), rows),
                  pl.BlockSpec((1, tm, sw), rows),
                  pl.BlockSpec((1, 1, mod3.shape[2]), lambda b, s: (b, 0, 0)),
                  pl.BlockSpec((1, aw), const),
                  pl.BlockSpec((aw + sw, d), const, pipeline_mode=single),
                  pl.BlockSpec((1, d), const),
                  pl.BlockSpec((d, 2 * dff), const, pipeline_mode=single),
                  pl.BlockSpec((CONV_WIDTH, dff), const),
                  pl.BlockSpec((1, dff), const),
                  pl.BlockSpec((dff, d), const, pipeline_mode=single),
                  pl.BlockSpec((1, d), const)],
        out_specs=pl.BlockSpec((1, tm, d), rows),
        scratch_shapes=[pltpu.VMEM((8, dff), F32)],
        compiler_params=pltpu.CompilerParams(
            dimension_semantics=("arbitrary", "arbitrary"), vmem_limit_bytes=VMEM_LIMIT),
        name="out_ffn",
    )(x, attn, ssm, mod3, g_attn, w_out, g_ffn, w_up, conv_w, conv_b, w_down, g_final)


def _block_diag(blocks):
    n, r, c = blocks.shape
    eye = jnp.eye(n, dtype=blocks.dtype)
    return (blocks[:, :, None, :] * eye[:, None, :, None]).reshape(n * r, n * c)


def _chunked_block_diag(blocks, per):
    g, r, c = blocks.shape
    return jax.vmap(_block_diag)(blocks.reshape(g // per, per, r, c))


def kernel(x, c, w_ada, b_ada, g_mix, w_in, b_fgate, a_re, a_im, log_dt, ssm_b_re, ssm_b_im,
           ssm_c_re, ssm_c_im, d_skip, w_glu, b_glu, g_attn_out, g_ssm_out, w_out, g_ffn, w_up,
           conv_w, conv_b, w_down, g_final):
    bsz, seq, d = x.shape
    depth = w_ada.shape[0]
    aw = N_HEADS * HEAD_DIM
    ngroups = a_re.shape[1]
    sw = ngroups * SSM_GROUP
    per = GROUPS_PER_CHUNK
    tm_in = min(TM_IN, seq)
    tq = min(TQ, seq)
    tm_ffn = min(TM_FFN, seq)
    tt = min(TT, seq)
    assert bsz % 8 == 0 and seq % tm_in == 0 and seq % tq == 0 and ngroups % per == 0
    assert tq % (2 * LANES) == 0
    assert seq % tt == 0 and tt % SUBLANES_BF16 == 0

    for l in range(depth):
        mod3 = _modulation(c, w_ada[l], b_ada[l]).reshape(bsz, 1, N_MOD * d)

        w = w_in[l]
        w_cat = jnp.concatenate([w[:, 0:3 * aw], w[:, 3 * aw + N_HEADS:]], axis=1).astype(BF16)
        wf_t = w[:, 3 * aw:3 * aw + N_HEADS].T.astype(BF16)
        q, k, v, f, u = _in_proj(x, mod3, g_mix[l].reshape(1, d), w_cat, wf_t,
                                 b_fgate[l].reshape(N_HEADS, 1), tm_in)

        attn = _attention(q, k, v, f, tq)

        l_re, l_im, bb_re, bb_im = _ssm_params(
            a_re[l], a_im[l], log_dt[l],
            ssm_b_re[l].transpose(0, 2, 1), ssm_b_im[l].transpose(0, 2, 1))
        bmat = jnp.concatenate([_chunked_block_diag(bb_re, per), _chunked_block_diag(bb_im, per)],
                               axis=-1).astype(BF16)
        cre = _chunked_block_diag(ssm_c_re[l].transpose(0, 2, 1), per).astype(BF16)
        cim = _chunked_block_diag(ssm_c_im[l].transpose(0, 2, 1), per).astype(BF16)
        wglu_bd = _block_diag(w_glu[l]).astype(BF16)
        ssm = _ssm(u, l_re.reshape(ngroups // per, per * STATE_DIM),
                   l_im.reshape(ngroups // per, per * STATE_DIM),
                   bmat, cre, cim, d_skip[l].reshape(1, sw), wglu_bd, b_glu[l].reshape(1, sw),
                   g_ssm_out[l].reshape(1, sw), tt)

        assert depth == 1
        x = _out_ffn(x, attn, ssm, mod3, g_attn_out[l].reshape(1, aw), w_out[l].astype(BF16),
                     g_ffn[l].reshape(1, d), w_up[l].astype(BF16), conv_w[l], conv_b[l].reshape(1, -1),
                     w_down[l].astype(BF16), g_final.reshape(1, d), tm_ffn)
    return x
```

```python
import functools
import math

import jax
import jax.numpy as jnp
from jax import lax
from jax.experimental import pallas as pl
from jax.experimental.pallas import tpu as pltpu

EPS = 1e-6
NEG_INF = -1e30
HEAD_DIM = 64
N_HEADS = 8
SSM_GROUP = 16
STATE_DIM = 64
CONV_WIDTH = 3
N_MOD = 6

LANES = 128
SUBLANES = 8
SUBLANES_BF16 = 16
GROUPS_PER_CHUNK = 8
VMEM_LIMIT = 56 * 1024 * 1024
TM_IN = 1024
TQ = 512
TM_FFN = 512
TT = 64
SSM_PIECE_STEPS = 8
LOG2E = math.log2(math.e)

BF16 = jnp.bfloat16
F32 = jnp.float32


def _split3(a):
    hi = a.astype(BF16)
    r1 = a - hi.astype(F32)
    mid = r1.astype(BF16)
    lo = (r1 - mid.astype(F32)).astype(BF16)
    return hi, mid, lo


def _rms_scale(x):
    return lax.rsqrt(jnp.mean(x * x, axis=-1, keepdims=True) + EPS)


def _mod_kernel(c_ref, w_ref, b_ref, o_ref):
    c = c_ref[...]
    s = c * jax.nn.sigmoid(c)
    w = w_ref[...]
    s_hi = s.astype(BF16)
    s_lo = (s - s_hi.astype(F32)).astype(BF16)
    w_hi = w.astype(BF16)
    w_lo = (w - w_hi.astype(F32)).astype(BF16)
    acc = jnp.dot(s_hi, w_hi, preferred_element_type=F32)
    acc += jnp.dot(s_lo, w_hi, preferred_element_type=F32)
    acc += jnp.dot(s_hi, w_lo, preferred_element_type=F32)
    o_ref[...] = acc + b_ref[...]


def _modulation(c, w_ada, b_ada):
    bsz, d = c.shape
    n = w_ada.shape[1]
    tn = 512
    return pl.pallas_call(
        _mod_kernel,
        out_shape=jax.ShapeDtypeStruct((bsz, n), F32),
        grid=(n // tn,),
        in_specs=[pl.BlockSpec((bsz, d), lambda j: (0, 0)),
                  pl.BlockSpec((d, tn), lambda j: (0, j)),
                  pl.BlockSpec((1, tn), lambda j: (0, j))],
        out_specs=pl.BlockSpec((bsz, tn), lambda j: (0, j)),
        compiler_params=pltpu.CompilerParams(dimension_semantics=("arbitrary",)),
        name="modulation",
    )(c, w_ada, b_ada.reshape(1, n))


def _inproj_kernel(x_ref, mod_ref, g_ref, w_ref, wft_ref, bf_ref, tri_ref,
                   q_ref, k_ref, v_ref, u_ref, carry_ref, *, d, aw):
    si = pl.program_id(1)
    tm = x_ref.shape[1]

    @pl.when(si == 0)
    def _():
        carry_ref[...] = jnp.zeros_like(carry_ref)

    x = x_ref[0]
    shift = mod_ref[0, :, 0:d]
    scale = mod_ref[0, :, d:2 * d]
    h = x * _rms_scale(x) * g_ref[...]
    h = h * (1.0 + scale) + shift
    hb = h.astype(BF16)

    proj = jnp.dot(hb, w_ref[...], preferred_element_type=F32)
    u_ref[0] = proj[:, 3 * aw:].astype(BF16)

    z = lax.dot_general(wft_ref[...], hb, (((1,), (1,)), ((), ())),
                        preferred_element_type=F32) + bf_ref[...]
    logf = jnp.minimum(z, 0.0) - jnp.log1p(jnp.exp(-jnp.abs(z)))
    pieces = jnp.concatenate(_split3(logf), axis=0)
    cs = jnp.dot(pieces, tri_ref[...], preferred_element_type=F32)
    nh = logf.shape[0]
    cum = cs[0:nh] + cs[nh:2 * nh] + cs[2 * nh:3 * nh]
    cum = cum + carry_ref[:, LANES - 1:LANES]
    carry_ref[...] = cum[:, tm - LANES:tm]

    hi, mid, lo = [p.astype(F32) for p in _split3(cum * LOG2E)]
    one = jnp.ones((1, tm), F32)
    sub = lax.broadcasted_iota(jnp.int32, (SUBLANES, tm), 0)
    lane = lax.broadcasted_iota(jnp.int32, (tm, LANES), 1)
    left = lane < HEAD_DIM
    gap = jnp.zeros((HEAD_DIM - SUBLANES, tm), F32)

    def tile(vals):
        t = jnp.zeros((SUBLANES, tm), F32)
        for i, val in enumerate(vals):
            t = jnp.where(sub == i, val, t)
        return t

    def q_tile(hd):
        return tile([hi[hd:hd + 1], mid[hd:hd + 1], lo[hd:hd + 1], one, one, one])

    def k_tile(hd):
        return tile([one, one, one, -hi[hd:hd + 1], -mid[hd:hd + 1], -lo[hd:hd + 1]])

    ones_first = jnp.where(lane == HEAD_DIM, 1.0, 0.0)
    ones_second = jnp.where(lane == 0, 1.0, 0.0)
    qscale = HEAD_DIM ** -0.5 * LOG2E
    for pair in range(aw // LANES):
        first, second = 2 * pair, 2 * pair + 1
        cols = slice(LANES * pair, LANES * (pair + 1))
        pq = proj[:, 0:aw][:, cols] * qscale
        pk = proj[:, aw:2 * aw][:, cols]
        pv = proj[:, 2 * aw:3 * aw][:, cols]
        qe = jnp.concatenate([q_tile(second), gap, q_tile(first), gap], axis=0).T
        ke = jnp.concatenate([k_tile(second), gap, k_tile(first), gap], axis=0).T
        q_ref[0, first] = jnp.where(left, pq, qe).astype(BF16)
        q_ref[0, second] = jnp.where(left, qe, pq).astype(BF16)
        k_ref[0, first] = jnp.where(left, pk, ke).astype(BF16)
        k_ref[0, second] = jnp.where(left, ke, pk).astype(BF16)
        v_ref[0, first] = jnp.where(left, pv, ones_first).astype(BF16)
        v_ref[0, second] = jnp.where(left, ones_second, pv).astype(BF16)


def _in_proj(x, mod3, g_mix, w_cat, wf_t, b_f, tm):
    bsz, seq, d = x.shape
    ncat = w_cat.shape[1]
    aw = N_HEADS * HEAD_DIM
    sw = ncat - 3 * aw
    nh = wf_t.shape[0]
    tri = (jnp.arange(tm)[:, None] <= jnp.arange(tm)[None, :]).astype(BF16)
    const = lambda b, s: (0, 0)
    rows = lambda b, s: (b, s, 0)
    heads = lambda b, s: (b, 0, s, 0)
    kern = functools.partial(_inproj_kernel, d=d, aw=aw)
    return pl.pallas_call(
        kern,
        out_shape=(jax.ShapeDtypeStruct((bsz, nh, seq, LANES), BF16),
                   jax.ShapeDtypeStruct((bsz, nh, seq, LANES), BF16),
                   jax.ShapeDtypeStruct((bsz, nh, seq, LANES), BF16),
                   jax.ShapeDtypeStruct((bsz, seq, sw), BF16)),
        grid=(bsz, seq // tm),
        in_specs=[pl.BlockSpec((1, tm, d), rows),
                  pl.BlockSpec((1, 1, mod3.shape[2]), lambda b, s: (b, 0, 0)),
                  pl.BlockSpec((1, d), const),
                  pl.BlockSpec((d, ncat), const, pipeline_mode=pl.Buffered(1)),
                  pl.BlockSpec((nh, d), const),
                  pl.BlockSpec((nh, 1), const),
                  pl.BlockSpec((tm, tm), const, pipeline_mode=pl.Buffered(1))],
        out_specs=(pl.BlockSpec((1, nh, tm, LANES), heads),
                   pl.BlockSpec((1, nh, tm, LANES), heads),
                   pl.BlockSpec((1, nh, tm, LANES), heads),
                   pl.BlockSpec((1, tm, sw), rows)),
        scratch_shapes=[pltpu.VMEM((nh, LANES), F32)],
        compiler_params=pltpu.CompilerParams(
            dimension_semantics=("arbitrary", "arbitrary"), vmem_limit_bytes=VMEM_LIMIT),
        name="in_proj",
    )(x, mod3, g_mix, w_cat, wf_t, b_f, tri)


def _attn_kernel(qx_ref, kx_ref, vx_ref, o_ref, m_ref, acc_ref, s_ref, *, tq):
    seq = kx_ref.shape[2]
    extra_lo = (HEAD_DIM, 0)

    row = lax.broadcasted_iota(jnp.int32, (tq // 2, tq // 2), 0)
    col = lax.broadcasted_iota(jnp.int32, (tq // 2, tq // 2), 1)
    causal = col <= row
    lane = lax.broadcasted_iota(jnp.int32, (tq, LANES), 1)

    hq = tq // 2

    def qk(q0, slot, j0, width, last):
        parts = ((0, hq, width - hq), (hq, hq, width)) if last else ((0, tq, width),)
        for hh in range(2):
            for r0, nr, wd in parts:
                qx = qx_ref[0, hh, q0 + r0:q0 + r0 + nr, :]
                kb = kx_ref[0, hh, j0:j0 + wd, :]
                s_ref[slot, hh, r0:r0 + nr, 0:wd] = lax.dot_general(
                    qx, kb, (((1,), (1,)), ((), ())), preferred_element_type=F32)

    def softmax_pv(par, slot, j0, width, last):
        parts = ((0, hq, width - hq), (hq, hq, width)) if last else ((0, tq, width),)
        for hh in range(2):
            for r0, nr, wd in parts:
                s = s_ref[slot, hh, r0:r0 + nr, 0:wd]
                if last:
                    tail = jnp.where(causal, s[:, wd - hq:], NEG_INF)
                    s = tail if wd == hq else jnp.concatenate([s[:, 0:wd - hq], tail], axis=1)
                m_old = m_ref[par, hh, r0:r0 + nr, :]
                m_new = jnp.maximum(m_old, jnp.max(s, axis=-1, keepdims=True))
                alpha = jnp.exp2(m_old - m_new)
                p = jnp.exp2(s - jnp.concatenate([m_new] * (wd // LANES), axis=1)).astype(BF16)
                vb = vx_ref[0, hh, j0:j0 + wd, :]
                acc_ref[par, hh, r0:r0 + nr, :] = (alpha * acc_ref[par, hh, r0:r0 + nr, :]
                                                   + jnp.dot(p, vb, preferred_element_type=F32))
                m_ref[par, hh, r0:r0 + nr, :] = m_new

    nq = seq // tq
    steps = []
    for qi in range(nq):
        nkeys = (qi + 1) * tq
        j0 = 0
        while j0 < nkeys:
            width = min(2 * tq, nkeys - j0)
            steps.append((qi, j0, width, j0 + width == nkeys))
            j0 += width
    qk(0, 0, steps[0][1], steps[0][2], steps[0][3])
    for i, (qi, j0, width, last) in enumerate(steps):
        slot = i % 2
        par = qi % 2
        q0 = qi * tq
        if j0 == 0:
            m_ref[par] = jnp.full(m_ref.shape[1:], NEG_INF, F32)
            acc_ref[par] = jnp.zeros(acc_ref.shape[1:], F32)
        if i + 1 < len(steps):
            nqi, nj0, nwidth, nlast = steps[i + 1]
            qk(nqi * tq, 1 - slot, nj0, nwidth, nlast)
        softmax_pv(par, slot, j0, width, last)
        if last:
            a0 = acc_ref[par, 0]
            a1 = acc_ref[par, 1]
            o0 = a0 / a0[:, extra_lo[0]:extra_lo[0] + 1]
            o1 = a1 / a1[:, extra_lo[1]:extra_lo[1] + 1]
            o_ref[0, q0:q0 + tq, :] = jnp.where(lane < HEAD_DIM, o0, o1).astype(BF16)


def _attention(qx, kx, vx, tq):
    bsz, nh, seq, _ = qx.shape
    npair = nh // 2
    kern = functools.partial(_attn_kernel, tq=tq)
    pair = lambda b, h: (b, h, 0, 0)
    return pl.pallas_call(
        kern,
        out_shape=jax.ShapeDtypeStruct((bsz, seq, nh * HEAD_DIM), BF16),
        grid=(bsz, npair),
        in_specs=[pl.BlockSpec((1, 2, seq, LANES), pair),
                  pl.BlockSpec((1, 2, seq, LANES), pair),
                  pl.BlockSpec((1, 2, seq, LANES), pair)],
        out_specs=pl.BlockSpec((1, seq, LANES), lambda b, h: (b, 0, h)),
        scratch_shapes=[pltpu.VMEM((2, 2, tq, LANES), F32),
                        pltpu.VMEM((2, 2, tq, LANES), F32),
                        pltpu.VMEM((2, 2, tq, 2 * tq), F32)],
        compiler_params=pltpu.CompilerParams(
            dimension_semantics=("arbitrary", "arbitrary"), vmem_limit_bytes=VMEM_LIMIT),
        name="fox_attention",
    )(qx, kx, vx)


def _ssm_param_kernel(are_ref, aim_ref, ldt_ref, bre_ref, bim_ref,
                      lre_ref, lim_ref, bbre_ref, bbim_ref):
    a_re = are_ref[...]
    a_im = aim_ref[...]
    dt = jnp.exp(ldt_ref[...])
    mag = jnp.exp(a_re * dt)
    ang = a_im * dt
    l_re = mag * jnp.cos(ang)
    l_im = mag * jnp.sin(ang)
    lre_ref[...] = l_re
    lim_ref[...] = l_im
    n_re = l_re - 1.0
    den = a_re * a_re + a_im * a_im
    c_re = (n_re * a_re + l_im * a_im) / den
    c_im = (l_im * a_re - n_re * a_im) / den
    c_re = c_re[:, None, :]
    c_im = c_im[:, None, :]
    b_re = bre_ref[...]
    b_im = bim_ref[...]
    bbre_ref[...] = c_re * b_re - c_im * b_im
    bbim_ref[...] = c_re * b_im + c_im * b_re


def _ssm_params(a_re, a_im, log_dt, b_re_t, b_im_t):
    g, p = a_re.shape
    c = b_re_t.shape[1]
    return pl.pallas_call(
        _ssm_param_kernel,
        out_shape=(jax.ShapeDtypeStruct((g, p), F32), jax.ShapeDtypeStruct((g, p), F32),
                   jax.ShapeDtypeStruct((g, c, p), F32), jax.ShapeDtypeStruct((g, c, p), F32)),
        name="ssm_params",
    )(a_re, a_im, log_dt.reshape(g, 1), b_re_t, b_im_t)


def _ssm_kernel(u_ref, perm_ref, permt_ref, lre_ref, lim_ref, bmat_ref, cre_ref, cim_ref, dskip_ref,
                wglu_ref, bglu_ref, gout_ref, o_ref, bu_ref, h_ref, sre_ref, sim_ref, *, nb, tt, th):
    nchunk = bmat_ref.shape[0]
    cw = bmat_ref.shape[2] // 2
    iw = bmat_ref.shape[1]
    w = u_ref.shape[2]

    @pl.when(pl.program_id(0) == 0)
    def _():
        sre_ref[...] = jnp.zeros_like(sre_ref)
        sim_ref[...] = jnp.zeros_like(sim_ref)

    parts = []
    for g in range(tt // th):
        ug = u_ref[:, g * th:(g + 1) * th, :].reshape(nb * th, w)
        parts.append(jnp.dot(perm_ref[...], ug, preferred_element_type=F32).astype(BF16))
    u = jnp.concatenate(parts, axis=0)

    spb = SSM_PIECE_STEPS
    rb = spb * nb
    npiece = tt // spb

    def bu(c, k):
        base = 2 * cw * c
        bu_ref[k * rb:(k + 1) * rb, base:base + 2 * cw] = jnp.dot(
            u[k * rb:(k + 1) * rb, iw * c:iw * (c + 1)], bmat_ref[c], preferred_element_type=F32)

    def readout(c, k):
        base = 2 * cw * c
        rows = slice(k * rb, (k + 1) * rb)
        y = jnp.dot(h_ref[rows, base:base + cw], cre_ref[c], preferred_element_type=F32)
        return y - jnp.dot(h_ref[rows, base + cw:base + 2 * cw], cim_ref[c], preferred_element_type=F32)

    ys = [[None] * npiece for _ in range(nchunk)]
    for k in range(npiece):
        bu(0, k)
    for c in range(nchunk + 1):
        if c < nchunk:
            base = 2 * cw * c
            a_re = jnp.broadcast_to(lre_ref[c:c + 1, :], (nb, cw))
            a_im = jnp.broadcast_to(lim_ref[c:c + 1, :], (nb, cw))
            h_re = sre_ref[:, cw * c:cw * (c + 1)]
            h_im = sim_ref[:, cw * c:cw * (c + 1)]
        for k in range(npiece):
            if c + 1 < nchunk:
                bu(c + 1, k)
            if c >= 1:
                ys[c - 1][k] = readout(c - 1, k)
            if c < nchunk:
                for t in range(k * spb, (k + 1) * spb):
                    r0 = t * nb
                    b_re = bu_ref[r0:r0 + nb, base:base + cw]
                    b_im = bu_ref[r0:r0 + nb, base + cw:base + 2 * cw]
                    h_re, h_im = (a_re * h_re - a_im * h_im + b_re, a_re * h_im + a_im * h_re + b_im)
                    h_ref[r0:r0 + nb, base:base + cw] = h_re.astype(BF16)
                    h_ref[r0:r0 + nb, base + cw:base + 2 * cw] = h_im.astype(BF16)
        if c < nchunk:
            sre_ref[:, cw * c:cw * (c + 1)] = h_re
            sim_ref[:, cw * c:cw * (c + 1)] = h_im
    ys = [jnp.concatenate(pieces, axis=0) for pieces in ys]
    y = jnp.concatenate(ys, axis=-1) + dskip_ref[...] * u.astype(F32)
    z = jax.nn.gelu(y, approximate=True)
    gate = jnp.dot(z.astype(BF16), wglu_ref[...], preferred_element_type=F32) + bglu_ref[...]
    out = z * jax.nn.sigmoid(gate)
    out = (out * _rms_scale(out) * gout_ref[...]).astype(BF16)
    for g in range(tt // th):
        og = jnp.dot(permt_ref[...], out[g * th * nb:(g + 1) * th * nb], preferred_element_type=F32)
        o_ref[:, g * th:(g + 1) * th, :] = og.astype(BF16).reshape(nb, th, w)


def _ssm(u, l_re, l_im, bmat, cre, cim, dskip, wglu_bd, bglu, g_out, tt):
    nb, seq, w = u.shape
    th = SUBLANES_BF16
    rows = nb * tt
    nchunk, iw, cw2 = bmat.shape
    cw = cw2 // 2
    src = (jnp.arange(nb)[None, :] * th + jnp.arange(th)[:, None]).reshape(-1)
    perm = (src[:, None] == jnp.arange(nb * th)[None, :]).astype(BF16)
    const2 = lambda i: (0, 0)
    const3 = lambda i: (0, 0, 0)
    kern = functools.partial(_ssm_kernel, nb=nb, tt=tt, th=th)
    return pl.pallas_call(
        kern,
        out_shape=jax.ShapeDtypeStruct((nb, seq, w), BF16),
        grid=(seq // tt,),
        in_specs=[pl.BlockSpec((nb, tt, w), lambda i: (0, i, 0)),
                  pl.BlockSpec((nb * th, nb * th), const2),
                  pl.BlockSpec((nb * th, nb * th), const2),
                  pl.BlockSpec((nchunk, cw), const2),
                  pl.BlockSpec((nchunk, cw), const2),
                  pl.BlockSpec((nchunk, iw, cw2), const3),
                  pl.BlockSpec((nchunk, cw, iw), const3),
                  pl.BlockSpec((nchunk, cw, iw), const3),
                  pl.BlockSpec((1, w), const2),
                  pl.BlockSpec((w, w), const2),
                  pl.BlockSpec((1, w), const2),
                  pl.BlockSpec((1, w), const2)],
        out_specs=pl.BlockSpec((nb, tt, w), lambda i: (0, i, 0)),
        scratch_shapes=[pltpu.VMEM((rows, nchunk * cw2), F32),
                        pltpu.VMEM((rows, nchunk * cw2), BF16),
                        pltpu.VMEM((nb, nchunk * cw), F32),
                        pltpu.VMEM((nb, nchunk * cw), F32)],
        compiler_params=pltpu.CompilerParams(
            dimension_semantics=("arbitrary",), vmem_limit_bytes=VMEM_LIMIT),
        name="s5_ssm",
    )(u, perm, perm.T, l_re, l_im, bmat, cre, cim, dskip, wglu_bd, bglu, g_out)


def _outffn_kernel(x_ref, attn_ref, ssm_ref, mod_ref, gattn_ref, wout_ref, gffn_ref, wup_ref,
                   convw_ref, convb_ref, wdown_ref, gfin_ref, o_ref, carry_ref, *, d, dff):
    si = pl.program_id(1)
    tm = x_ref.shape[1]

    @pl.when(si == 0)
    def _():
        carry_ref[...] = jnp.zeros_like(carry_ref)

    gt_m = mod_ref[0, :, 2 * d:3 * d]
    sh_f = mod_ref[0, :, 3 * d:4 * d]
    sc_f = mod_ref[0, :, 4 * d:5 * d]
    gt_f = mod_ref[0, :, 5 * d:6 * d]

    a = attn_ref[0].astype(F32)
    an = (a * _rms_scale(a) * gattn_ref[...]).astype(BF16)
    mixin = jnp.concatenate([an, ssm_ref[0]], axis=-1)
    mix = jnp.dot(mixin, wout_ref[...], preferred_element_type=F32)
    x1 = x_ref[0] + gt_m * mix

    h = x1 * _rms_scale(x1) * gffn_ref[...]
    h = (h * (1.0 + sc_f) + sh_f).astype(BF16)
    up = jnp.dot(h, wup_ref[...], preferred_element_type=F32)
    gp = up[:, 0:dff]
    val = up[:, dff:2 * dff]

    prev = carry_ref[...]
    r8 = lax.broadcasted_iota(jnp.int32, prev.shape, 0)

    def shifted(k):
        body = pltpu.roll(gp, k, 0)
        head = jnp.where(r8 < k, pltpu.roll(prev, k, 0), body[0:8])
        return jnp.concatenate([head, body[8:]], axis=0)

    conv = (convw_ref[0:1, :] * shifted(2) + convw_ref[1:2, :] * shifted(1)
            + convw_ref[2:3, :] * gp + convb_ref[...])
    carry_ref[...] = gp[tm - 8:tm]
    act = (conv * jax.nn.sigmoid(conv) * val).astype(BF16)
    y = jnp.dot(act, wdown_ref[...], preferred_element_type=F32)
    x2 = x1 + gt_f * y
    o_ref[0] = x2 * _rms_scale(x2) * gfin_ref[...]


def _out_ffn(x, attn, ssm, mod3, g_attn, w_out, g_ffn, w_up, conv_w, conv_b, w_down, g_final, tm):
    bsz, seq, d = x.shape
    aw = attn.shape[2]
    sw = ssm.shape[2]
    dff = w_down.shape[0]
    const = lambda b, s: (0, 0)
    rows = lambda b, s: (b, s, 0)
    single = pl.Buffered(1)
    kern = functools.partial(_outffn_kernel, d=d, dff=dff)
    return pl.pallas_call(
        kern,
        out_shape=jax.ShapeDtypeStruct((bsz, seq, d), F32),
        grid=(bsz, seq // tm),
        in_specs=[pl.BlockSpec((1, tm, d), rows),
                  pl.BlockSpec((1, tm, aw), rows),
                  pl.BlockSpec((1, tm, sw), rows),
                  pl.BlockSpec((1, 1, mod3.shape[2]), lambda b, s: (b, 0, 0)),
                  pl.BlockSpec((1, aw), const),
                  pl.BlockSpec((aw + sw, d), const, pipeline_mode=single),
                  pl.BlockSpec((1, d), const),
                  pl.BlockSpec((d, 2 * dff), const, pipeline_mode=single),
                  pl.BlockSpec((CONV_WIDTH, dff), const),
                  pl.BlockSpec((1, dff), const),
                  pl.BlockSpec((dff, d), const, pipeline_mode=single),
                  pl.BlockSpec((1, d), const)],
        out_specs=pl.BlockSpec((1, tm, d), rows),
        scratch_shapes=[pltpu.VMEM((8, dff), F32)],
        compiler_params=pltpu.CompilerParams(
            dimension_semantics=("arbitrary", "arbitrary"), vmem_limit_bytes=VMEM_LIMIT),
        name="out_ffn",
    )(x, attn, ssm, mod3, g_attn, w_out, g_ffn, w_up, conv_w, conv_b, w_down, g_final)


def _block_diag(blocks):
    n, r, c = blocks.shape
    eye = jnp.eye(n, dtype=blocks.dtype)
    return (blocks[:, :, None, :] * eye[:, None, :, None]).reshape(n * r, n * c)


def _chunked_block_diag(blocks, per):
    g, r, c = blocks.shape
    return jax.vmap(_block_diag)(blocks.reshape(g // per, per, r, c))


def kernel(x, c, w_ada, b_ada, g_mix, w_in, b_fgate, a_re, a_im, log_dt, ssm_b_re, ssm_b_im,
           ssm_c_re, ssm_c_im, d_skip, w_glu, b_glu, g_attn_out, g_ssm_out, w_out, g_ffn, w_up,
           conv_w, conv_b, w_down, g_final):
    bsz, seq, d = x.shape
    depth = w_ada.shape[0]
    aw = N_HEADS * HEAD_DIM
    ngroups = a_re.shape[1]
    sw = ngroups * SSM_GROUP
    per = GROUPS_PER_CHUNK
    tm_in = min(TM_IN, seq)
    tq = min(TQ, seq)
    tm_ffn = min(TM_FFN, seq)
    tt = min(TT, seq)
    assert bsz % 8 == 0 and seq % tm_in == 0 and seq % tq == 0 and ngroups % per == 0
    assert tq % (2 * LANES) == 0
    assert seq % tt == 0 and tt % SUBLANES_BF16 == 0

    for l in range(depth):
        mod3 = _modulation(c, w_ada[l], b_ada[l]).reshape(bsz, 1, N_MOD * d)

        w = w_in[l]
        w_cat = jnp.concatenate([w[:, 0:3 * aw], w[:, 3 * aw + N_HEADS:]], axis=1).astype(BF16)
        wf_t = w[:, 3 * aw:3 * aw + N_HEADS].T.astype(BF16)
        qx, kx, vx, u = _in_proj(x, mod3, g_mix[l].reshape(1, d), w_cat, wf_t,
                                 b_fgate[l].reshape(N_HEADS, 1), tm_in)

        attn = _attention(qx, kx, vx, tq)

        l_re, l_im, bb_re, bb_im = _ssm_params(
            a_re[l], a_im[l], log_dt[l],
            ssm_b_re[l].transpose(0, 2, 1), ssm_b_im[l].transpose(0, 2, 1))
        bmat = jnp.concatenate([_chunked_block_diag(bb_re, per), _chunked_block_diag(bb_im, per)],
                               axis=-1).astype(BF16)
        cre = _chunked_block_diag(ssm_c_re[l].transpose(0, 2, 1), per).astype(BF16)
        cim = _chunked_block_diag(ssm_c_im[l].transpose(0, 2, 1), per).astype(BF16)
        wglu_bd = _block_diag(w_glu[l]).astype(BF16)
        ssm = _ssm(u, l_re.reshape(ngroups // per, per * STATE_DIM),
                   l_im.reshape(ngroups // per, per * STATE_DIM),
                   bmat, cre, cim, d_skip[l].reshape(1, sw), wglu_bd, b_glu[l].reshape(1, sw),
                   g_ssm_out[l].reshape(1, sw), tt)

        assert depth == 1
        x = _out_ffn(x, attn, ssm, mod3, g_attn_out[l].reshape(1, aw), w_out[l].astype(BF16),
                     g_ffn[l].reshape(1, d), w_up[l].astype(BF16), conv_w[l], conv_b[l].reshape(1, -1),
                     w_down[l].astype(BF16), g_final.reshape(1, d), tm_ffn)
    return x
```

```python
import functools
import math

import jax
import jax.numpy as jnp
from jax import lax
from jax.experimental import pallas as pl
from jax.experimental.pallas import tpu as pltpu

EPS = 1e-6
NEG_INF = -1e30
HEAD_DIM = 64
N_HEADS = 8
SSM_GROUP = 16
STATE_DIM = 64
CONV_WIDTH = 3
N_MOD = 6

LANES = 128
SUBLANES = 8
SUBLANES_BF16 = 16
GROUPS_PER_CHUNK = 8
VMEM_LIMIT = 56 * 1024 * 1024
TM_IN = 1024
IN_SPLIT = 4
TQ = 512
TM_FFN = 512
TT = 64
SSM_PIECE_STEPS = 8
LOG2E = math.log2(math.e)

BF16 = jnp.bfloat16
F32 = jnp.float32


def _split3(a):
    hi = a.astype(BF16)
    r1 = a - hi.astype(F32)
    mid = r1.astype(BF16)
    lo = (r1 - mid.astype(F32)).astype(BF16)
    return hi, mid, lo


def _rms_scale(x):
    return lax.rsqrt(jnp.mean(x * x, axis=-1, keepdims=True) + EPS)


def _mod_kernel(c_ref, w_ref, b_ref, o_ref):
    c = c_ref[...]
    s = c * jax.nn.sigmoid(c)
    w = w_ref[...]
    s_hi = s.astype(BF16)
    s_lo = (s - s_hi.astype(F32)).astype(BF16)
    w_hi = w.astype(BF16)
    w_lo = (w - w_hi.astype(F32)).astype(BF16)
    acc = jnp.dot(s_hi, w_hi, preferred_element_type=F32)
    acc += jnp.dot(s_lo, w_hi, preferred_element_type=F32)
    acc += jnp.dot(s_hi, w_lo, preferred_element_type=F32)
    o_ref[...] = acc + b_ref[...]


def _modulation(c, w_ada, b_ada):
    bsz, d = c.shape
    n = w_ada.shape[1]
    tn = 512
    return pl.pallas_call(
        _mod_kernel,
        out_shape=jax.ShapeDtypeStruct((bsz, n), F32),
        grid=(n // tn,),
        in_specs=[pl.BlockSpec((bsz, d), lambda j: (0, 0)),
                  pl.BlockSpec((d, tn), lambda j: (0, j)),
                  pl.BlockSpec((1, tn), lambda j: (0, j))],
        out_specs=pl.BlockSpec((bsz, tn), lambda j: (0, j)),
        compiler_params=pltpu.CompilerParams(dimension_semantics=("arbitrary",)),
        name="modulation",
    )(c, w_ada, b_ada.reshape(1, n))


def _inproj_kernel(x_ref, mod_ref, g_ref, w_ref, wft_ref, bf_ref, tri_ref,
                   q_ref, k_ref, v_ref, u_ref, carry_ref, *, d, aw):
    si = pl.program_id(1)
    tm = x_ref.shape[1]

    @pl.when(si == 0)
    def _():
        carry_ref[...] = jnp.zeros_like(carry_ref)

    shift = mod_ref[0, :, 0:d]
    scale = mod_ref[0, :, d:2 * d]
    nblk = IN_SPLIT
    th = tm // nblk

    def normed(r):
        x = x_ref[0, r * th:(r + 1) * th, :]
        h = x * _rms_scale(x) * g_ref[...]
        return (h * (1.0 + scale) + shift).astype(BF16)

    one = jnp.ones((1, th), F32)
    sub = lax.broadcasted_iota(jnp.int32, (SUBLANES, th), 0)
    lane = lax.broadcasted_iota(jnp.int32, (th, LANES), 1)
    left = lane < HEAD_DIM
    gap = jnp.zeros((HEAD_DIM - SUBLANES, th), F32)
    ones_first = jnp.where(lane == HEAD_DIM, 1.0, 0.0)
    ones_second = jnp.where(lane == 0, 1.0, 0.0)
    qscale = HEAD_DIM ** -0.5 * LOG2E
    step = 2 * LANES
    sw = u_ref.shape[2]
    nh = bf_ref.shape[0]

    def tile(vals):
        t = jnp.zeros((SUBLANES, th), F32)
        for i, val in enumerate(vals):
            t = jnp.where(sub == i, val, t)
        return t

    def gate_logits(hb):
        z = lax.dot_general(wft_ref[...], hb, (((1,), (1,)), ((), ())),
                            preferred_element_type=F32) + bf_ref[...]
        logf = jnp.minimum(z, 0.0) - jnp.log1p(jnp.exp(-jnp.abs(z)))
        return jnp.concatenate(_split3(logf), axis=0)

    def prefix(pieces, carry):
        cs = jnp.dot(pieces, tri_ref[...], preferred_element_type=F32)
        return cs[0:nh] + cs[nh:2 * nh] + cs[2 * nh:3 * nh] + carry

    hb_next = normed(0)
    cum_next = prefix(gate_logits(hb_next), carry_ref[:, LANES - 1:LANES])
    for r in range(nblk):
        hb, cum = hb_next, cum_next
        rows = slice(r * th, (r + 1) * th)
        if r + 1 < nblk:
            hb_next = normed(r + 1)
            pieces_next = gate_logits(hb_next)
        hi, mid, lo = [p.astype(F32) for p in _split3(cum * LOG2E)]

        def q_tile(hd, hi=hi, mid=mid, lo=lo):
            return tile([hi[hd:hd + 1], mid[hd:hd + 1], lo[hd:hd + 1], one, one, one])

        def k_tile(hd, hi=hi, mid=mid, lo=lo):
            return tile([one, one, one, -hi[hd:hd + 1], -mid[hd:hd + 1], -lo[hd:hd + 1]])

        def piece(c0, width, hb=hb):
            return jnp.dot(hb, w_ref[:, c0:c0 + width], preferred_element_type=F32)

        for c0 in range(0, aw, step):
            pq2 = piece(c0, step) * qscale
            pk2 = piece(aw + c0, step)
            pv2 = piece(2 * aw + c0, step)
            for half in range(2):
                pair = c0 // LANES + half
                first, second = 2 * pair, 2 * pair + 1
                cols = slice(LANES * half, LANES * (half + 1))
                pq, pk, pv = pq2[:, cols], pk2[:, cols], pv2[:, cols]
                qe = jnp.concatenate([q_tile(second), gap, q_tile(first), gap], axis=0).T
                ke = jnp.concatenate([k_tile(second), gap, k_tile(first), gap], axis=0).T
                q_ref[0, first, rows, :] = jnp.where(left, pq, qe).astype(BF16)
                q_ref[0, second, rows, :] = jnp.where(left, qe, pq).astype(BF16)
                k_ref[0, first, rows, :] = jnp.where(left, pk, ke).astype(BF16)
                k_ref[0, second, rows, :] = jnp.where(left, ke, pk).astype(BF16)
                v_ref[0, first, rows, :] = jnp.where(left, pv, ones_first).astype(BF16)
                v_ref[0, second, rows, :] = jnp.where(left, ones_second, pv).astype(BF16)
        for c0 in range(0, sw, step):
            u_ref[0, rows, c0:c0 + step] = piece(3 * aw + c0, step).astype(BF16)
        if r + 1 < nblk:
            cum_next = prefix(pieces_next, cum[:, th - 1:th])
    carry_ref[...] = jnp.broadcast_to(cum[:, th - 1:th], carry_ref.shape)


def _in_proj(x, mod3, g_mix, w_cat, wf_t, b_f, tm):
    bsz, seq, d = x.shape
    ncat = w_cat.shape[1]
    aw = N_HEADS * HEAD_DIM
    sw = ncat - 3 * aw
    nh = wf_t.shape[0]
    tb = tm // IN_SPLIT
    tri = (jnp.arange(tb)[:, None] <= jnp.arange(tb)[None, :]).astype(BF16)
    const = lambda b, s: (0, 0)
    rows = lambda b, s: (b, s, 0)
    heads = lambda b, s: (b, 0, s, 0)
    kern = functools.partial(_inproj_kernel, d=d, aw=aw)
    return pl.pallas_call(
        kern,
        out_shape=(jax.ShapeDtypeStruct((bsz, nh, seq, LANES), BF16),
                   jax.ShapeDtypeStruct((bsz, nh, seq, LANES), BF16),
                   jax.ShapeDtypeStruct((bsz, nh, seq, LANES), BF16),
                   jax.ShapeDtypeStruct((bsz, seq, sw), BF16)),
        grid=(bsz, seq // tm),
        in_specs=[pl.BlockSpec((1, tm, d), rows),
                  pl.BlockSpec((1, 1, mod3.shape[2]), lambda b, s: (b, 0, 0)),
                  pl.BlockSpec((1, d), const),
                  pl.BlockSpec((d, ncat), const, pipeline_mode=pl.Buffered(1)),
                  pl.BlockSpec((nh, d), const),
                  pl.BlockSpec((nh, 1), const),
                  pl.BlockSpec((tb, tb), const, pipeline_mode=pl.Buffered(1))],
        out_specs=(pl.BlockSpec((1, nh, tm, LANES), heads),
                   pl.BlockSpec((1, nh, tm, LANES), heads),
                   pl.BlockSpec((1, nh, tm, LANES), heads),
                   pl.BlockSpec((1, tm, sw), rows)),
        scratch_shapes=[pltpu.VMEM((nh, LANES), F32)],
        compiler_params=pltpu.CompilerParams(
            dimension_semantics=("arbitrary", "arbitrary"), vmem_limit_bytes=VMEM_LIMIT),
        name="in_proj",
    )(x, mod3, g_mix, w_cat, wf_t, b_f, tri)


def _attn_kernel(qx_ref, kx_ref, vx_ref, o_ref, m_ref, acc_ref, s_ref, *, tq):
    seq = kx_ref.shape[2]
    extra_lo = (HEAD_DIM, 0)

    row = lax.broadcasted_iota(jnp.int32, (tq // 2, tq // 2), 0)
    col = lax.broadcasted_iota(jnp.int32, (tq // 2, tq // 2), 1)
    causal = col <= row
    lane = lax.broadcasted_iota(jnp.int32, (tq, LANES), 1)

    hq = tq // 2

    def qk(q0, slot, j0, width, last):
        parts = ((0, hq, width - hq), (hq, hq, width)) if last else ((0, tq, width),)
        for hh in range(2):
            for r0, nr, wd in parts:
                qx = qx_ref[0, hh, q0 + r0:q0 + r0 + nr, :]
                kb = kx_ref[0, hh, j0:j0 + wd, :]
                s_ref[slot, hh, r0:r0 + nr, 0:wd] = lax.dot_general(
                    qx, kb, (((1,), (1,)), ((), ())), preferred_element_type=F32)

    def softmax_pv(par, slot, j0, width, last):
        parts = ((0, hq, width - hq), (hq, hq, width)) if last else ((0, tq, width),)
        for hh in range(2):
            for r0, nr, wd in parts:
                s = s_ref[slot, hh, r0:r0 + nr, 0:wd]
                if last:
                    tail = jnp.where(causal, s[:, wd - hq:], NEG_INF)
                    s = tail if wd == hq else jnp.concatenate([s[:, 0:wd - hq], tail], axis=1)
                m_old = m_ref[par, hh, r0:r0 + nr, :]
                m_new = jnp.maximum(m_old, jnp.max(s, axis=-1, keepdims=True))
                alpha = jnp.exp2(m_old - m_new)
                p = jnp.exp2(s - jnp.concatenate([m_new] * (wd // LANES), axis=1)).astype(BF16)
                vb = vx_ref[0, hh, j0:j0 + wd, :]
                acc_ref[par, hh, r0:r0 + nr, :] = (alpha * acc_ref[par, hh, r0:r0 + nr, :]
                                                   + jnp.dot(p, vb, preferred_element_type=F32))
                m_ref[par, hh, r0:r0 + nr, :] = m_new

    nq = seq // tq
    steps = []
    for qi in range(nq):
        nkeys = (qi + 1) * tq
        j0 = 0
        while j0 < nkeys:
            width = min(2 * tq, nkeys - j0)
            steps.append((qi, j0, width, j0 + width == nkeys))
            j0 += width
    qk(0, 0, steps[0][1], steps[0][2], steps[0][3])
    for i, (qi, j0, width, last) in enumerate(steps):
        slot = i % 2
        par = qi % 2
        q0 = qi * tq
        if j0 == 0:
            m_ref[par] = jnp.full(m_ref.shape[1:], NEG_INF, F32)
            acc_ref[par] = jnp.zeros(acc_ref.shape[1:], F32)
        if i + 1 < len(steps):
            nqi, nj0, nwidth, nlast = steps[i + 1]
            qk(nqi * tq, 1 - slot, nj0, nwidth, nlast)
        softmax_pv(par, slot, j0, width, last)
        if last:
            a0 = acc_ref[par, 0]
            a1 = acc_ref[par, 1]
            o0 = a0 / a0[:, extra_lo[0]:extra_lo[0] + 1]
            o1 = a1 / a1[:, extra_lo[1]:extra_lo[1] + 1]
            o_ref[0, q0:q0 + tq, :] = jnp.where(lane < HEAD_DIM, o0, o1).astype(BF16)


def _attention(qx, kx, vx, tq):
    bsz, nh, seq, _ = qx.shape
    npair = nh // 2
    kern = functools.partial(_attn_kernel, tq=tq)
    pair = lambda b, h: (b, h, 0, 0)
    return pl.pallas_call(
        kern,
        out_shape=jax.ShapeDtypeStruct((bsz, seq, nh * HEAD_DIM), BF16),
        grid=(bsz, npair),
        in_specs=[pl.BlockSpec((1, 2, seq, LANES), pair),
                  pl.BlockSpec((1, 2, seq, LANES), pair),
                  pl.BlockSpec((1, 2, seq, LANES), pair)],
        out_specs=pl.BlockSpec((1, seq, LANES), lambda b, h: (b, 0, h)),
        scratch_shapes=[pltpu.VMEM((2, 2, tq, LANES), F32),
                        pltpu.VMEM((2, 2, tq, LANES), F32),
                        pltpu.VMEM((2, 2, tq, 2 * tq), F32)],
        compiler_params=pltpu.CompilerParams(
            dimension_semantics=("arbitrary", "arbitrary"), vmem_limit_bytes=VMEM_LIMIT),
        name="fox_attention",
    )(qx, kx, vx)


def _ssm_param_kernel(are_ref, aim_ref, ldt_ref, bre_ref, bim_ref,
                      lre_ref, lim_ref, bbre_ref, bbim_ref):
    a_re = are_ref[...]
    a_im = aim_ref[...]
    dt = jnp.exp(ldt_ref[...])
    mag = jnp.exp(a_re * dt)
    ang = a_im * dt
    l_re = mag * jnp.cos(ang)
    l_im = mag * jnp.sin(ang)
    lre_ref[...] = l_re
    lim_ref[...] = l_im
    n_re = l_re - 1.0
    den = a_re * a_re + a_im * a_im
    c_re = (n_re * a_re + l_im * a_im) / den
    c_im = (l_im * a_re - n_re * a_im) / den
    c_re = c_re[:, None, :]
    c_im = c_im[:, None, :]
    b_re = bre_ref[...]
    b_im = bim_ref[...]
    bbre_ref[...] = c_re * b_re - c_im * b_im
    bbim_ref[...] = c_re * b_im + c_im * b_re


def _ssm_params(a_re, a_im, log_dt, b_re_t, b_im_t):
    g, p = a_re.shape
    c = b_re_t.shape[1]
    return pl.pallas_call(
        _ssm_param_kernel,
        out_shape=(jax.ShapeDtypeStruct((g, p), F32), jax.ShapeDtypeStruct((g, p), F32),
                   jax.ShapeDtypeStruct((g, c, p), F32), jax.ShapeDtypeStruct((g, c, p), F32)),
        name="ssm_params",
    )(a_re, a_im, log_dt.reshape(g, 1), b_re_t, b_im_t)


def _ssm_kernel(u_ref, perm_ref, permt_ref, lre_ref, lim_ref, bmat_ref, cre_ref, cim_ref, dskip_ref,
                wglu_ref, bglu_ref, gout_ref, o_ref, bu_ref, h_ref, sre_ref, sim_ref, *, nb, tt, th):
    nchunk = bmat_ref.shape[0]
    cw = bmat_ref.shape[2] // 2
    iw = bmat_ref.shape[1]
    w = u_ref.shape[2]

    @pl.when(pl.program_id(0) == 0)
    def _():
        sre_ref[...] = jnp.zeros_like(sre_ref)
        sim_ref[...] = jnp.zeros_like(sim_ref)

    parts = []
    for g in range(tt // th):
        ug = u_ref[:, g * th:(g + 1) * th, :].reshape(nb * th, w)
        parts.append(jnp.dot(perm_ref[...], ug, preferred_element_type=F32).astype(BF16))
    u = jnp.concatenate(parts, axis=0)

    spb = SSM_PIECE_STEPS
    rb = spb * nb
    npiece = tt // spb

    def bu(c, k):
        base = 2 * cw * c
        bu_ref[k * rb:(k + 1) * rb, base:base + 2 * cw] = jnp.dot(
            u[k * rb:(k + 1) * rb, iw * c:iw * (c + 1)], bmat_ref[c], preferred_element_type=F32)

    def readout(c, k):
        base = 2 * cw * c
        rows = slice(k * rb, (k + 1) * rb)
        y = jnp.dot(h_ref[rows, base:base + cw], cre_ref[c], preferred_element_type=F32)
        return y - jnp.dot(h_ref[rows, base + cw:base + 2 * cw], cim_ref[c], preferred_element_type=F32)

    ys = [[None] * npiece for _ in range(nchunk)]
    for k in range(npiece):
        bu(0, k)
    for c in range(nchunk + 1):
        if c < nchunk:
            base = 2 * cw * c
            a_re = jnp.broadcast_to(lre_ref[c:c + 1, :], (nb, cw))
            a_im = jnp.broadcast_to(lim_ref[c:c + 1, :], (nb, cw))
            h_re = sre_ref[:, cw * c:cw * (c + 1)]
            h_im = sim_ref[:, cw * c:cw * (c + 1)]
        for k in range(npiece):
            if c + 1 < nchunk:
                bu(c + 1, k)
            if c >= 1:
                ys[c - 1][k] = readout(c - 1, k)
            if c < nchunk:
                for t in range(k * spb, (k + 1) * spb):
                    r0 = t * nb
                    b_re = bu_ref[r0:r0 + nb, base:base + cw]
                    b_im = bu_ref[r0:r0 + nb, base + cw:base + 2 * cw]
                    h_re, h_im = (a_re * h_re - a_im * h_im + b_re, a_re * h_im + a_im * h_re + b_im)
                    h_ref[r0:r0 + nb, base:base + cw] = h_re.astype(BF16)
                    h_ref[r0:r0 + nb, base + cw:base + 2 * cw] = h_im.astype(BF16)
        if c < nchunk:
            sre_ref[:, cw * c:cw * (c + 1)] = h_re
            sim_ref[:, cw * c:cw * (c + 1)] = h_im
    ys = [jnp.concatenate(pieces, axis=0) for pieces in ys]
    y = jnp.concatenate(ys, axis=-1) + dskip_ref[...] * u.astype(F32)
    z = jax.nn.gelu(y, approximate=True)
    gate = jnp.dot(z.astype(BF16), wglu_ref[...], preferred_element_type=F32) + bglu_ref[...]
    out = z * jax.nn.sigmoid(gate)
    out = (out * _rms_scale(out) * gout_ref[...]).astype(BF16)
    for g in range(tt // th):
        og = jnp.dot(permt_ref[...], out[g * th * nb:(g + 1) * th * nb], preferred_element_type=F32)
        o_ref[:, g * th:(g + 1) * th, :] = og.astype(BF16).reshape(nb, th, w)


def _ssm(u, l_re, l_im, bmat, cre, cim, dskip, wglu_bd, bglu, g_out, tt):
    nb, seq, w = u.shape
    th = SUBLANES_BF16
    rows = nb * tt
    nchunk, iw, cw2 = bmat.shape
    cw = cw2 // 2
    src = (jnp.arange(nb)[None, :] * th + jnp.arange(th)[:, None]).reshape(-1)
    perm = (src[:, None] == jnp.arange(nb * th)[None, :]).astype(BF16)
    const2 = lambda i: (0, 0)
    const3 = lambda i: (0, 0, 0)
    kern = functools.partial(_ssm_kernel, nb=nb, tt=tt, th=th)
    return pl.pallas_call(
        kern,
        out_shape=jax.ShapeDtypeStruct((nb, seq, w), BF16),
        grid=(seq // tt,),
        in_specs=[pl.BlockSpec((nb, tt, w), lambda i: (0, i, 0)),
                  pl.BlockSpec((nb * th, nb * th), const2),
                  pl.BlockSpec((nb * th, nb * th), const2),
                  pl.BlockSpec((nchunk, cw), const2),
                  pl.BlockSpec((nchunk, cw), const2),
                  pl.BlockSpec((nchunk, iw, cw2), const3),
                  pl.BlockSpec((nchunk, cw, iw), const3),
                  pl.BlockSpec((nchunk, cw, iw), const3),
                  pl.BlockSpec((1, w), const2),
                  pl.BlockSpec((w, w), const2),
                  pl.BlockSpec((1, w), const2),
                  pl.BlockSpec((1, w), const2)],
        out_specs=pl.BlockSpec((nb, tt, w), lambda i: (0, i, 0)),
        scratch_shapes=[pltpu.VMEM((rows, nchunk * cw2), F32),
                        pltpu.VMEM((rows, nchunk * cw2), BF16),
                        pltpu.VMEM((nb, nchunk * cw), F32),
                        pltpu.VMEM((nb, nchunk * cw), F32)],
        compiler_params=pltpu.CompilerParams(
            dimension_semantics=("arbitrary",), vmem_limit_bytes=VMEM_LIMIT),
        name="s5_ssm",
    )(u, perm, perm.T, l_re, l_im, bmat, cre, cim, dskip, wglu_bd, bglu, g_out)


def _outffn_kernel(x_ref, attn_ref, ssm_ref, mod_ref, gattn_ref, wout_ref, gffn_ref, wup_ref,
                   convw_ref, convb_ref, wdown_ref, gfin_ref, o_ref, carry_ref, *, d, dff):
    si = pl.program_id(1)
    tm = x_ref.shape[1]

    @pl.when(si == 0)
    def _():
        carry_ref[...] = jnp.zeros_like(carry_ref)

    gt_m = mod_ref[0, :, 2 * d:3 * d]
    sh_f = mod_ref[0, :, 3 * d:4 * d]
    sc_f = mod_ref[0, :, 4 * d:5 * d]
    gt_f = mod_ref[0, :, 5 * d:6 * d]

    a = attn_ref[0].astype(F32)
    an = (a * _rms_scale(a) * gattn_ref[...]).astype(BF16)
    mixin = jnp.concatenate([an, ssm_ref[0]], axis=-1)
    mix = jnp.dot(mixin, wout_ref[...], preferred_element_type=F32)
    x1 = x_ref[0] + gt_m * mix

    h = x1 * _rms_scale(x1) * gffn_ref[...]
    h = (h * (1.0 + sc_f) + sh_f).astype(BF16)
    up = jnp.dot(h, wup_ref[...], preferred_element_type=F32)
    gp = up[:, 0:dff]
    val = up[:, dff:2 * dff]

    prev = carry_ref[...]
    r8 = lax.broadcasted_iota(jnp.int32, prev.shape, 0)

    def shifted(k):
        body = pltpu.roll(gp, k, 0)
        head = jnp.where(r8 < k, pltpu.roll(prev, k, 0), body[0:8])
        return jnp.concatenate([head, body[8:]], axis=0)

    conv = (convw_ref[0:1, :] * shifted(2) + convw_ref[1:2, :] * shifted(1)
            + convw_ref[2:3, :] * gp + convb_ref[...])
    carry_ref[...] = gp[tm - 8:tm]
    act = (conv * jax.nn.sigmoid(conv) * val).astype(BF16)
    y = jnp.dot(act, wdown_ref[...], preferred_element_type=F32)
    x2 = x1 + gt_f * y
    o_ref[0] = x2 * _rms_scale(x2) * gfin_ref[...]


def _out_ffn(x, attn, ssm, mod3, g_attn, w_out, g_ffn, w_up, conv_w, conv_b, w_down, g_final, tm):
    bsz, seq, d = x.shape
    aw = attn.shape[2]
    sw = ssm.shape[2]
    dff = w_down.shape[0]
    const = lambda b, s: (0, 0)
    rows = lambda b, s: (b, s, 0)
    single = pl.Buffered(1)
    kern = functools.partial(_outffn_kernel, d=d, dff=dff)
    return pl.pallas_call(
        kern,
        out_shape=jax.ShapeDtypeStruct((bsz, seq, d), F32),
        grid=(bsz, seq // tm),
        in_specs=[pl.BlockSpec((1, tm, d), rows),
                  pl.BlockSpec((1, tm, aw), rows),
                  pl.BlockSpec((1, tm, sw), rows),
                  pl.BlockSpec((1, 1, mod3.shape[2]), lambda b, s: (b, 0, 0)),
                  pl.BlockSpec((1, aw), const),
                  pl.BlockSpec((aw + sw, d), const, pipeline_mode=single),
                  pl.BlockSpec((1, d), const),
                  pl.BlockSpec((d, 2 * dff), const, pipeline_mode=single),
                  pl.BlockSpec((CONV_WIDTH, dff), const),
                  pl.BlockSpec((1, dff), const),
                  pl.BlockSpec((dff, d), const, pipeline_mode=single),
                  pl.BlockSpec((1, d), const)],
        out_specs=pl.BlockSpec((1, tm, d), rows),
        scratch_shapes=[pltpu.VMEM((8, dff), F32)],
        compiler_params=pltpu.CompilerParams(
            dimension_semantics=("arbitrary", "arbitrary"), vmem_limit_bytes=VMEM_LIMIT),
        name="out_ffn",
    )(x, attn, ssm, mod3, g_attn, w_out, g_ffn, w_up, conv_w, conv_b, w_down, g_final)


def _block_diag(blocks):
    n, r, c = blocks.shape
    eye = jnp.eye(n, dtype=blocks.dtype)
    return (blocks[:, :, None, :] * eye[:, None, :, None]).reshape(n * r, n * c)


def _chunked_block_diag(blocks, per):
    g, r, c = blocks.shape
    return jax.vmap(_block_diag)(blocks.reshape(g // per, per, r, c))


def kernel(x, c, w_ada, b_ada, g_mix, w_in, b_fgate, a_re, a_im, log_dt, ssm_b_re, ssm_b_im,
           ssm_c_re, ssm_c_im, d_skip, w_glu, b_glu, g_attn_out, g_ssm_out, w_out, g_ffn, w_up,
           conv_w, conv_b, w_down, g_final):
    bsz, seq, d = x.shape
    depth = w_ada.shape[0]
    aw = N_HEADS * HEAD_DIM
    ngroups = a_re.shape[1]
    sw = ngroups * SSM_GROUP
    per = GROUPS_PER_CHUNK
    tm_in = min(TM_IN, seq)
    tq = min(TQ, seq)
    tm_ffn = min(TM_FFN, seq)
    tt = min(TT, seq)
    assert bsz % 8 == 0 and seq % tm_in == 0 and seq % tq == 0 and ngroups % per == 0
    assert tq % (2 * LANES) == 0
    assert seq % tt == 0 and tt % SUBLANES_BF16 == 0

    for l in range(depth):
        mod3 = _modulation(c, w_ada[l], b_ada[l]).reshape(bsz, 1, N_MOD * d)

        w = w_in[l]
        w_cat = jnp.concatenate([w[:, 0:3 * aw], w[:, 3 * aw + N_HEADS:]], axis=1).astype(BF16)
        wf_t = w[:, 3 * aw:3 * aw + N_HEADS].T.astype(BF16)
        qx, kx, vx, u = _in_proj(x, mod3, g_mix[l].reshape(1, d), w_cat, wf_t,
                                 b_fgate[l].reshape(N_HEADS, 1), tm_in)

        attn = _attention(qx, kx, vx, tq)

        l_re, l_im, bb_re, bb_im = _ssm_params(
            a_re[l], a_im[l], log_dt[l],
            ssm_b_re[l].transpose(0, 2, 1), ssm_b_im[l].transpose(0, 2, 1))
        bmat = jnp.concatenate([_chunked_block_diag(bb_re, per), _chunked_block_diag(bb_im, per)],
                               axis=-1).astype(BF16)
        cre = _chunked_block_diag(ssm_c_re[l].transpose(0, 2, 1), per).astype(BF16)
        cim = _chunked_block_diag(ssm_c_im[l].transpose(0, 2, 1), per).astype(BF16)
        wglu_bd = _block_diag(w_glu[l]).astype(BF16)
        ssm = _ssm(u, l_re.reshape(ngroups // per, per * STATE_DIM),
                   l_im.reshape(ngroups // per, per * STATE_DIM),
                   bmat, cre, cim, d_skip[l].reshape(1, sw), wglu_bd, b_glu[l].reshape(1, sw),
                   g_ssm_out[l].reshape(1, sw), tt)

        assert depth == 1
        x = _out_ffn(x, attn, ssm, mod3, g_attn_out[l].reshape(1, aw), w_out[l].astype(BF16),
                     g_ffn[l].reshape(1, d), w_up[l].astype(BF16), conv_w[l], conv_b[l].reshape(1, -1),
                     w_down[l].astype(BF16), g_final.reshape(1, d), tm_ffn)
    return x
```

```python
import functools
import math

import jax
import jax.numpy as jnp
from jax import lax
from jax.experimental import pallas as pl
from jax.experimental.pallas import tpu as pltpu

EPS = 1e-6
NEG_INF = -1e30
HEAD_DIM = 64
N_HEADS = 8
SSM_GROUP = 16
STATE_DIM = 64
CONV_WIDTH = 3
N_MOD = 6

LANES = 128
SUBLANES = 8
SUBLANES_BF16 = 16
GROUPS_PER_CHUNK = 8
VMEM_LIMIT = 56 * 1024 * 1024
TM_IN = 1024
IN_SPLIT = 4
TQ = 512
ATTN_SUPER = 4
TM_FFN = 512
TT = 64
SSM_PIECE_STEPS = 8
LOG2E = math.log2(math.e)

BF16 = jnp.bfloat16
F32 = jnp.float32


def _split3(a):
    hi = a.astype(BF16)
    r1 = a - hi.astype(F32)
    mid = r1.astype(BF16)
    lo = (r1 - mid.astype(F32)).astype(BF16)
    return hi, mid, lo


def _rms_scale(x):
    return lax.rsqrt(jnp.mean(x * x, axis=-1, keepdims=True) + EPS)


def _mod_kernel(c_ref, w_ref, b_ref, o_ref):
    c = c_ref[...]
    s = c * jax.nn.sigmoid(c)
    w = w_ref[...]
    s_hi = s.astype(BF16)
    s_lo = (s - s_hi.astype(F32)).astype(BF16)
    w_hi = w.astype(BF16)
    w_lo = (w - w_hi.astype(F32)).astype(BF16)
    acc = jnp.dot(s_hi, w_hi, preferred_element_type=F32)
    acc += jnp.dot(s_lo, w_hi, preferred_element_type=F32)
    acc += jnp.dot(s_hi, w_lo, preferred_element_type=F32)
    o_ref[...] = acc + b_ref[...]


def _modulation(c, w_ada, b_ada):
    bsz, d = c.shape
    n = w_ada.shape[1]
    tn = 512
    return pl.pallas_call(
        _mod_kernel,
        out_shape=jax.ShapeDtypeStruct((bsz, n), F32),
        grid=(n // tn,),
        in_specs=[pl.BlockSpec((bsz, d), lambda j: (0, 0)),
                  pl.BlockSpec((d, tn), lambda j: (0, j)),
                  pl.BlockSpec((1, tn), lambda j: (0, j))],
        out_specs=pl.BlockSpec((bsz, tn), lambda j: (0, j)),
        compiler_params=pltpu.CompilerParams(dimension_semantics=("arbitrary",)),
        name="modulation",
    )(c, w_ada, b_ada.reshape(1, n))


def _inproj_kernel(x_ref, mod_ref, g_ref, w_ref, wft_ref, bf_ref, tri_ref,
                   q_ref, k_ref, v_ref, u_ref, carry_ref, *, d, aw):
    si = pl.program_id(1)
    tm = x_ref.shape[1]

    @pl.when(si == 0)
    def _():
        carry_ref[...] = jnp.zeros_like(carry_ref)

    shift = mod_ref[0, :, 0:d]
    scale = mod_ref[0, :, d:2 * d]
    nblk = IN_SPLIT
    th = tm // nblk

    def normed(r):
        x = x_ref[0, r * th:(r + 1) * th, :]
        h = x * _rms_scale(x) * g_ref[...]
        return (h * (1.0 + scale) + shift).astype(BF16)

    one = jnp.ones((1, th), F32)
    sub = lax.broadcasted_iota(jnp.int32, (SUBLANES, th), 0)
    lane = lax.broadcasted_iota(jnp.int32, (th, LANES), 1)
    left = lane < HEAD_DIM
    gap = jnp.zeros((HEAD_DIM - SUBLANES, th), F32)
    ones_first = jnp.where(lane == HEAD_DIM, 1.0, 0.0)
    ones_second = jnp.where(lane == 0, 1.0, 0.0)
    qscale = HEAD_DIM ** -0.5 * LOG2E
    step = 2 * LANES
    sw = u_ref.shape[2]
    nh = bf_ref.shape[0]

    def tile(vals):
        t = jnp.zeros((SUBLANES, th), F32)
        for i, val in enumerate(vals):
            t = jnp.where(sub == i, val, t)
        return t

    def gate_logits(hb):
        z = lax.dot_general(wft_ref[...], hb, (((1,), (1,)), ((), ())),
                            preferred_element_type=F32) + bf_ref[...]
        logf = jnp.minimum(z, 0.0) - jnp.log1p(jnp.exp(-jnp.abs(z)))
        return jnp.concatenate(_split3(logf), axis=0)

    def prefix(pieces, carry):
        cs = jnp.dot(pieces, tri_ref[...], preferred_element_type=F32)
        return cs[0:nh] + cs[nh:2 * nh] + cs[2 * nh:3 * nh] + carry

    hb_next = normed(0)
    cum_next = prefix(gate_logits(hb_next), carry_ref[:, LANES - 1:LANES])
    for r in range(nblk):
        hb, cum = hb_next, cum_next
        rows = slice(r * th, (r + 1) * th)
        if r + 1 < nblk:
            hb_next = normed(r + 1)
            pieces_next = gate_logits(hb_next)
        hi, mid, lo = [p.astype(F32) for p in _split3(cum * LOG2E)]

        def q_tile(hd, hi=hi, mid=mid, lo=lo):
            return tile([hi[hd:hd + 1], mid[hd:hd + 1], lo[hd:hd + 1], one, one, one])

        def k_tile(hd, hi=hi, mid=mid, lo=lo):
            return tile([one, one, one, -hi[hd:hd + 1], -mid[hd:hd + 1], -lo[hd:hd + 1]])

        def piece(c0, width, hb=hb):
            return jnp.dot(hb, w_ref[:, c0:c0 + width], preferred_element_type=F32)

        for c0 in range(0, aw, step):
            pq2 = piece(c0, step) * qscale
            pk2 = piece(aw + c0, step)
            pv2 = piece(2 * aw + c0, step)
            for half in range(2):
                pair = c0 // LANES + half
                first, second = 2 * pair, 2 * pair + 1
                cols = slice(LANES * half, LANES * (half + 1))
                pq, pk, pv = pq2[:, cols], pk2[:, cols], pv2[:, cols]
                qe = jnp.concatenate([q_tile(second), gap, q_tile(first), gap], axis=0).T
                ke = jnp.concatenate([k_tile(second), gap, k_tile(first), gap], axis=0).T
                q_ref[0, first, rows, :] = jnp.where(left, pq, qe).astype(BF16)
                q_ref[0, second, rows, :] = jnp.where(left, qe, pq).astype(BF16)
                k_ref[0, first, rows, :] = jnp.where(left, pk, ke).astype(BF16)
                k_ref[0, second, rows, :] = jnp.where(left, ke, pk).astype(BF16)
                v_ref[0, first, rows, :] = jnp.where(left, pv, ones_first).astype(BF16)
                v_ref[0, second, rows, :] = jnp.where(left, ones_second, pv).astype(BF16)
        for c0 in range(0, sw, step):
            u_ref[0, rows, c0:c0 + step] = piece(3 * aw + c0, step).astype(BF16)
        if r + 1 < nblk:
            cum_next = prefix(pieces_next, cum[:, th - 1:th])
    carry_ref[...] = jnp.broadcast_to(cum[:, th - 1:th], carry_ref.shape)


def _in_proj(x, mod3, g_mix, w_cat, wf_t, b_f, tm):
    bsz, seq, d = x.shape
    ncat = w_cat.shape[1]
    aw = N_HEADS * HEAD_DIM
    sw = ncat - 3 * aw
    nh = wf_t.shape[0]
    tb = tm // IN_SPLIT
    tri = (jnp.arange(tb)[:, None] <= jnp.arange(tb)[None, :]).astype(BF16)
    const = lambda b, s: (0, 0)
    rows = lambda b, s: (b, s, 0)
    heads = lambda b, s: (b, 0, s, 0)
    kern = functools.partial(_inproj_kernel, d=d, aw=aw)
    return pl.pallas_call(
        kern,
        out_shape=(jax.ShapeDtypeStruct((bsz, nh, seq, LANES), BF16),
                   jax.ShapeDtypeStruct((bsz, nh, seq, LANES), BF16),
                   jax.ShapeDtypeStruct((bsz, nh, seq, LANES), BF16),
                   jax.ShapeDtypeStruct((bsz, seq, sw), BF16)),
        grid=(bsz, seq // tm),
        in_specs=[pl.BlockSpec((1, tm, d), rows),
                  pl.BlockSpec((1, 1, mod3.shape[2]), lambda b, s: (b, 0, 0)),
                  pl.BlockSpec((1, d), const),
                  pl.BlockSpec((d, ncat), const, pipeline_mode=pl.Buffered(1)),
                  pl.BlockSpec((nh, d), const),
                  pl.BlockSpec((nh, 1), const),
                  pl.BlockSpec((tb, tb), const, pipeline_mode=pl.Buffered(1))],
        out_specs=(pl.BlockSpec((1, nh, tm, LANES), heads),
                   pl.BlockSpec((1, nh, tm, LANES), heads),
                   pl.BlockSpec((1, nh, tm, LANES), heads),
                   pl.BlockSpec((1, tm, sw), rows)),
        scratch_shapes=[pltpu.VMEM((nh, LANES), F32)],
        compiler_params=pltpu.CompilerParams(
            dimension_semantics=("arbitrary", "arbitrary"), vmem_limit_bytes=VMEM_LIMIT),
        name="in_proj",
    )(x, mod3, g_mix, w_cat, wf_t, b_f, tri)


def _attn_kernel(qx_ref, kx_ref, vx_ref, o_ref, m_ref, acc_ref, s_ref, *, tq):
    seq = kx_ref.shape[2]
    extra_lo = (HEAD_DIM, 0)

    row = lax.broadcasted_iota(jnp.int32, (tq // 2, tq // 2), 0)
    col = lax.broadcasted_iota(jnp.int32, (tq // 2, tq // 2), 1)
    causal = col <= row
    lane = lax.broadcasted_iota(jnp.int32, (tq, LANES), 1)

    hq = tq // 2

    def qk(q0, slot, j0, width, last):
        parts = ((0, hq, width - hq), (hq, hq, width)) if last else ((0, tq, width),)
        for hh in range(2):
            for r0, nr, wd in parts:
                qx = qx_ref[0, hh, q0 + r0:q0 + r0 + nr, :]
                kb = kx_ref[0, hh, j0:j0 + wd, :]
                s_ref[slot, hh, r0:r0 + nr, 0:wd] = lax.dot_general(
                    qx, kb, (((1,), (1,)), ((), ())), preferred_element_type=F32)

    def softmax_pv(par, slot, j0, width, last):
        parts = ((0, hq, width - hq), (hq, hq, width)) if last else ((0, tq, width),)
        for hh in range(2):
            for r0, nr, wd in parts:
                s = s_ref[slot, hh, r0:r0 + nr, 0:wd]
                if last:
                    tail = jnp.where(causal, s[:, wd - hq:], NEG_INF)
                    s = tail if wd == hq else jnp.concatenate([s[:, 0:wd - hq], tail], axis=1)
                m_old = m_ref[par, hh, r0:r0 + nr, :]
                m_new = jnp.maximum(m_old, jnp.max(s, axis=-1, keepdims=True))
                alpha = jnp.exp2(m_old - m_new)
                p = jnp.exp2(s - jnp.concatenate([m_new] * (wd // LANES), axis=1)).astype(BF16)
                vb = vx_ref[0, hh, j0:j0 + wd, :]
                acc_ref[par, hh, r0:r0 + nr, :] = (alpha * acc_ref[par, hh, r0:r0 + nr, :]
                                                   + jnp.dot(p, vb, preferred_element_type=F32))
                m_ref[par, hh, r0:r0 + nr, :] = m_new

    nq = seq // tq
    steps = []
    for qi in range(nq):
        nkeys = (qi + 1) * tq
        j0 = 0
        while j0 < nkeys:
            width = min(ATTN_SUPER * tq, nkeys - j0)
            steps.append((qi, j0, width, j0 + width == nkeys))
            j0 += width
    qk(0, 0, steps[0][1], steps[0][2], steps[0][3])
    for i, (qi, j0, width, last) in enumerate(steps):
        slot = i % 2
        par = qi % 2
        q0 = qi * tq
        if j0 == 0:
            m_ref[par] = jnp.full(m_ref.shape[1:], NEG_INF, F32)
            acc_ref[par] = jnp.zeros(acc_ref.shape[1:], F32)
        if i + 1 < len(steps):
            nqi, nj0, nwidth, nlast = steps[i + 1]
            qk(nqi * tq, 1 - slot, nj0, nwidth, nlast)
        softmax_pv(par, slot, j0, width, last)
        if last:
            a0 = acc_ref[par, 0]
            a1 = acc_ref[par, 1]
            o0 = a0 / a0[:, extra_lo[0]:extra_lo[0] + 1]
            o1 = a1 / a1[:, extra_lo[1]:extra_lo[1] + 1]
            o_ref[0, q0:q0 + tq, :] = jnp.where(lane < HEAD_DIM, o0, o1).astype(BF16)


def _attention(qx, kx, vx, tq):
    bsz, nh, seq, _ = qx.shape
    npair = nh // 2
    kern = functools.partial(_attn_kernel, tq=tq)
    pair = lambda b, h: (b, h, 0, 0)
    return pl.pallas_call(
        kern,
        out_shape=jax.ShapeDtypeStruct((bsz, seq, nh * HEAD_DIM), BF16),
        grid=(bsz, npair),
        in_specs=[pl.BlockSpec((1, 2, seq, LANES), pair),
                  pl.BlockSpec((1, 2, seq, LANES), pair),
                  pl.BlockSpec((1, 2, seq, LANES), pair)],
        out_specs=pl.BlockSpec((1, seq, LANES), lambda b, h: (b, 0, h)),
        scratch_shapes=[pltpu.VMEM((2, 2, tq, LANES), F32),
                        pltpu.VMEM((2, 2, tq, LANES), F32),
                        pltpu.VMEM((2, 2, tq, ATTN_SUPER * tq), F32)],
        compiler_params=pltpu.CompilerParams(
            dimension_semantics=("arbitrary", "arbitrary"), vmem_limit_bytes=VMEM_LIMIT),
        name="fox_attention",
    )(qx, kx, vx)


def _ssm_param_kernel(are_ref, aim_ref, ldt_ref, bre_ref, bim_ref,
                      lre_ref, lim_ref, bbre_ref, bbim_ref):
    a_re = are_ref[...]
    a_im = aim_ref[...]
    dt = jnp.exp(ldt_ref[...])
    mag = jnp.exp(a_re * dt)
    ang = a_im * dt
    l_re = mag * jnp.cos(ang)
    l_im = mag * jnp.sin(ang)
    lre_ref[...] = l_re
    lim_ref[...] = l_im
    n_re = l_re - 1.0
    den = a_re * a_re + a_im * a_im
    c_re = (n_re * a_re + l_im * a_im) / den
    c_im = (l_im * a_re - n_re * a_im) / den
    c_re = c_re[:, None, :]
    c_im = c_im[:, None, :]
    b_re = bre_ref[...]
    b_im = bim_ref[...]
    bbre_ref[...] = c_re * b_re - c_im * b_im
    bbim_ref[...] = c_re * b_im + c_im * b_re


def _ssm_params(a_re, a_im, log_dt, b_re_t, b_im_t):
    g, p = a_re.shape
    c = b_re_t.shape[1]
    return pl.pallas_call(
        _ssm_param_kernel,
        out_shape=(jax.ShapeDtypeStruct((g, p), F32), jax.ShapeDtypeStruct((g, p), F32),
                   jax.ShapeDtypeStruct((g, c, p), F32), jax.ShapeDtypeStruct((g, c, p), F32)),
        name="ssm_params",
    )(a_re, a_im, log_dt.reshape(g, 1), b_re_t, b_im_t)


def _ssm_kernel(u_ref, perm_ref, permt_ref, lre_ref, lim_ref, bmat_ref, cre_ref, cim_ref, dskip_ref,
                wglu_ref, bglu_ref, gout_ref, o_ref, bu_ref, h_ref, sre_ref, sim_ref, *, nb, tt, th):
    nchunk = bmat_ref.shape[0]
    cw = bmat_ref.shape[2] // 2
    iw = bmat_ref.shape[1]
    w = u_ref.shape[2]

    @pl.when(pl.program_id(0) == 0)
    def _():
        sre_ref[...] = jnp.zeros_like(sre_ref)
        sim_ref[...] = jnp.zeros_like(sim_ref)

    parts = []
    for g in range(tt // th):
        ug = u_ref[:, g * th:(g + 1) * th, :].reshape(nb * th, w)
        parts.append(jnp.dot(perm_ref[...], ug, preferred_element_type=F32).astype(BF16))
    u = jnp.concatenate(parts, axis=0)

    spb = SSM_PIECE_STEPS
    rb = spb * nb
    npiece = tt // spb

    def bu(c, k):
        base = 2 * cw * c
        bu_ref[k * rb:(k + 1) * rb, base:base + 2 * cw] = jnp.dot(
            u[k * rb:(k + 1) * rb, iw * c:iw * (c + 1)], bmat_ref[c], preferred_element_type=F32)

    def readout(c, k):
        base = 2 * cw * c
        rows = slice(k * rb, (k + 1) * rb)
        y = jnp.dot(h_ref[rows, base:base + cw], cre_ref[c], preferred_element_type=F32)
        return y - jnp.dot(h_ref[rows, base + cw:base + 2 * cw], cim_ref[c], preferred_element_type=F32)

    ys = [[None] * npiece for _ in range(nchunk)]
    for k in range(npiece):
        bu(0, k)
    for c in range(nchunk + 1):
        if c < nchunk:
            base = 2 * cw * c
            a_re = jnp.broadcast_to(lre_ref[c:c + 1, :], (nb, cw))
            a_im = jnp.broadcast_to(lim_ref[c:c + 1, :], (nb, cw))
            h_re = sre_ref[:, cw * c:cw * (c + 1)]
            h_im = sim_ref[:, cw * c:cw * (c + 1)]
        for k in range(npiece):
            if c + 1 < nchunk:
                bu(c + 1, k)
            if c >= 1:
                ys[c - 1][k] = readout(c - 1, k)
            if c < nchunk:
                for t in range(k * spb, (k + 1) * spb):
                    r0 = t * nb
                    b_re = bu_ref[r0:r0 + nb, base:base + cw]
                    b_im = bu_ref[r0:r0 + nb, base + cw:base + 2 * cw]
                    h_re, h_im = (a_re * h_re - a_im * h_im + b_re, a_re * h_im + a_im * h_re + b_im)
                    h_ref[r0:r0 + nb, base:base + cw] = h_re.astype(BF16)
                    h_ref[r0:r0 + nb, base + cw:base + 2 * cw] = h_im.astype(BF16)
        if c < nchunk:
            sre_ref[:, cw * c:cw * (c + 1)] = h_re
            sim_ref[:, cw * c:cw * (c + 1)] = h_im
    ys = [jnp.concatenate(pieces, axis=0) for pieces in ys]
    y = jnp.concatenate(ys, axis=-1) + dskip_ref[...] * u.astype(F32)
    z = jax.nn.gelu(y, approximate=True)
    gate = jnp.dot(z.astype(BF16), wglu_ref[...], preferred_element_type=F32) + bglu_ref[...]
    out = z * jax.nn.sigmoid(gate)
    out = (out * _rms_scale(out) * gout_ref[...]).astype(BF16)
    for g in range(tt // th):
        og = jnp.dot(permt_ref[...], out[g * th * nb:(g + 1) * th * nb], preferred_element_type=F32)
        o_ref[:, g * th:(g + 1) * th, :] = og.astype(BF16).reshape(nb, th, w)


def _ssm(u, l_re, l_im, bmat, cre, cim, dskip, wglu_bd, bglu, g_out, tt):
    nb, seq, w = u.shape
    th = SUBLANES_BF16
    rows = nb * tt
    nchunk, iw, cw2 = bmat.shape
    cw = cw2 // 2
    src = (jnp.arange(nb)[None, :] * th + jnp.arange(th)[:, None]).reshape(-1)
    perm = (src[:, None] == jnp.arange(nb * th)[None, :]).astype(BF16)
    const2 = lambda i: (0, 0)
    const3 = lambda i: (0, 0, 0)
    kern = functools.partial(_ssm_kernel, nb=nb, tt=tt, th=th)
    return pl.pallas_call(
        kern,
        out_shape=jax.ShapeDtypeStruct((nb, seq, w), BF16),
        grid=(seq // tt,),
        in_specs=[pl.BlockSpec((nb, tt, w), lambda i: (0, i, 0)),
                  pl.BlockSpec((nb * th, nb * th), const2),
                  pl.BlockSpec((nb * th, nb * th), const2),
                  pl.BlockSpec((nchunk, cw), const2),
                  pl.BlockSpec((nchunk, cw), const2),
                  pl.BlockSpec((nchunk, iw, cw2), const3),
                  pl.BlockSpec((nchunk, cw, iw), const3),
                  pl.BlockSpec((nchunk, cw, iw), const3),
                  pl.BlockSpec((1, w), const2),
                  pl.BlockSpec((w, w), const2),
                  pl.BlockSpec((1, w), const2),
                  pl.BlockSpec((1, w), const2)],
        out_specs=pl.BlockSpec((nb, tt, w), lambda i: (0, i, 0)),
        scratch_shapes=[pltpu.VMEM((rows, nchunk * cw2), F32),
                        pltpu.VMEM((rows, nchunk * cw2), BF16),
                        pltpu.VMEM((nb, nchunk * cw), F32),
                        pltpu.VMEM((nb, nchunk * cw), F32)],
        compiler_params=pltpu.CompilerParams(
            dimension_semantics=("arbitrary",), vmem_limit_bytes=VMEM_LIMIT),
        name="s5_ssm",
    )(u, perm, perm.T, l_re, l_im, bmat, cre, cim, dskip, wglu_bd, bglu, g_out)


def _outffn_kernel(x_ref, attn_ref, ssm_ref, mod_ref, gattn_ref, wout_ref, gffn_ref, wup_ref,
                   convw_ref, convb_ref, wdown_ref, gfin_ref, o_ref, carry_ref, *, d, dff):
    si = pl.program_id(1)
    tm = x_ref.shape[1]

    @pl.when(si == 0)
    def _():
        carry_ref[...] = jnp.zeros_like(carry_ref)

    gt_m = mod_ref[0, :, 2 * d:3 * d]
    sh_f = mod_ref[0, :, 3 * d:4 * d]
    sc_f = mod_ref[0, :, 4 * d:5 * d]
    gt_f = mod_ref[0, :, 5 * d:6 * d]

    a = attn_ref[0].astype(F32)
    an = (a * _rms_scale(a) * gattn_ref[...]).astype(BF16)
    mixin = jnp.concatenate([an, ssm_ref[0]], axis=-1)
    mix = jnp.dot(mixin, wout_ref[...], preferred_element_type=F32)
    x1 = x_ref[0] + gt_m * mix

    h = x1 * _rms_scale(x1) * gffn_ref[...]
    h = (h * (1.0 + sc_f) + sh_f).astype(BF16)
    up = jnp.dot(h, wup_ref[...], preferred_element_type=F32)
    gp = up[:, 0:dff]
    val = up[:, dff:2 * dff]

    prev = carry_ref[...]
    r8 = lax.broadcasted_iota(jnp.int32, prev.shape, 0)

    def shifted(k):
        body = pltpu.roll(gp, k, 0)
        head = jnp.where(r8 < k, pltpu.roll(prev, k, 0), body[0:8])
        return jnp.concatenate([head, body[8:]], axis=0)

    conv = (convw_ref[0:1, :] * shifted(2) + convw_ref[1:2, :] * shifted(1)
            + convw_ref[2:3, :] * gp + convb_ref[...])
    carry_ref[...] = gp[tm - 8:tm]
    act = (conv * jax.nn.sigmoid(conv) * val).astype(BF16)
    y = jnp.dot(act, wdown_ref[...], preferred_element_type=F32)
    x2 = x1 + gt_f * y
    o_ref[0] = x2 * _rms_scale(x2) * gfin_ref[...]


def _out_ffn(x, attn, ssm, mod3, g_attn, w_out, g_ffn, w_up, conv_w, conv_b, w_down, g_final, tm):
    bsz, seq, d = x.shape
    aw = attn.shape[2]
    sw = ssm.shape[2]
    dff = w_down.shape[0]
    const = lambda b, s: (0, 0)
    rows = lambda b, s: (b, s, 0)
    single = pl.Buffered(1)
    kern = functools.partial(_outffn_kernel, d=d, dff=dff)
    return pl.pallas_call(
        kern,
        out_shape=jax.ShapeDtypeStruct((bsz, seq, d), F32),
        grid=(bsz, seq // tm),
        in_specs=[pl.BlockSpec((1, tm, d), rows),
                  pl.BlockSpec((1, tm, aw), rows),
                  pl.BlockSpec((1, tm, sw), rows),
                  pl.BlockSpec((1, 1, mod3.shape[2]), lambda b, s: (b, 0, 0)),
                  pl.BlockSpec((1, aw), const),
                  pl.BlockSpec((aw + sw, d), const, pipeline_mode=single),
                  pl.BlockSpec((1, d), const),
                  pl.BlockSpec((d, 2 * dff), const, pipeline_mode=single),
                  pl.BlockSpec((CONV_WIDTH, dff), const),
                  pl.BlockSpec((1, dff), const),
                  pl.BlockSpec((dff, d), const, pipeline_mode=single),
                  pl.BlockSpec((1, d), const)],
        out_specs=pl.BlockSpec((1, tm, d), rows),
        scratch_shapes=[pltpu.VMEM((8, dff), F32)],
        compiler_params=pltpu.CompilerParams(
            dimension_semantics=("arbitrary", "arbitrary"), vmem_limit_bytes=VMEM_LIMIT),
        name="out_ffn",
    )(x, attn, ssm, mod3, g_attn, w_out, g_ffn, w_up, conv_w, conv_b, w_down, g_final)


def _block_diag(blocks):
    n, r, c = blocks.shape
    eye = jnp.eye(n, dtype=blocks.dtype)
    return (blocks[:, :, None, :] * eye[:, None, :, None]).reshape(n * r, n * c)


def _chunked_block_diag(blocks, per):
    g, r, c = blocks.shape
    return jax.vmap(_block_diag)(blocks.reshape(g // per, per, r, c))


def kernel(x, c, w_ada, b_ada, g_mix, w_in, b_fgate, a_re, a_im, log_dt, ssm_b_re, ssm_b_im,
           ssm_c_re, ssm_c_im, d_skip, w_glu, b_glu, g_attn_out, g_ssm_out, w_out, g_ffn, w_up,
           conv_w, conv_b, w_down, g_final):
    bsz, seq, d = x.shape
    depth = w_ada.shape[0]
    aw = N_HEADS * HEAD_DIM
    ngroups = a_re.shape[1]
    sw = ngroups * SSM_GROUP
    per = GROUPS_PER_CHUNK
    tm_in = min(TM_IN, seq)
    tq = min(TQ, seq)
    tm_ffn = min(TM_FFN, seq)
    tt = min(TT, seq)
    assert bsz % 8 == 0 and seq % tm_in == 0 and seq % tq == 0 and ngroups % per == 0
    assert tq % (2 * LANES) == 0
    assert seq % tt == 0 and tt % SUBLANES_BF16 == 0

    for l in range(depth):
        mod3 = _modulation(c, w_ada[l], b_ada[l]).reshape(bsz, 1, N_MOD * d)

        w = w_in[l]
        w_cat = jnp.concatenate([w[:, 0:3 * aw], w[:, 3 * aw + N_HEADS:]], axis=1).astype(BF16)
        wf_t = w[:, 3 * aw:3 * aw + N_HEADS].T.astype(BF16)
        qx, kx, vx, u = _in_proj(x, mod3, g_mix[l].reshape(1, d), w_cat, wf_t,
                                 b_fgate[l].reshape(N_HEADS, 1), tm_in)

        attn = _attention(qx, kx, vx, tq)

        l_re, l_im, bb_re, bb_im = _ssm_params(
            a_re[l], a_im[l], log_dt[l],
            ssm_b_re[l].transpose(0, 2, 1), ssm_b_im[l].transpose(0, 2, 1))
        bmat = jnp.concatenate([_chunked_block_diag(bb_re, per), _chunked_block_diag(bb_im, per)],
                               axis=-1).astype(BF16)
        cre = _chunked_block_diag(ssm_c_re[l].transpose(0, 2, 1), per).astype(BF16)
        cim = _chunked_block_diag(ssm_c_im[l].transpose(0, 2, 1), per).astype(BF16)
        wglu_bd = _block_diag(w_glu[l]).astype(BF16)
        ssm = _ssm(u, l_re.reshape(ngroups // per, per * STATE_DIM),
                   l_im.reshape(ngroups // per, per * STATE_DIM),
                   bmat, cre, cim, d_skip[l].reshape(1, sw), wglu_bd, b_glu[l].reshape(1, sw),
                   g_ssm_out[l].reshape(1, sw), tt)

        assert depth == 1
        x = _out_ffn(x, attn, ssm, mod3, g_attn_out[l].reshape(1, aw), w_out[l].astype(BF16),
                     g_ffn[l].reshape(1, d), w_up[l].astype(BF16), conv_w[l], conv_b[l].reshape(1, -1),
                     w_down[l].astype(BF16), g_final.reshape(1, d), tm_ffn)
    return x
```

```python
import functools
import math

import jax
import jax.numpy as jnp
from jax import lax
from jax.experimental import pallas as pl
from jax.experimental.pallas import tpu as pltpu

EPS = 1e-6
NEG_INF = -1e30
HEAD_DIM = 64
N_HEADS = 8
SSM_GROUP = 16
STATE_DIM = 64
CONV_WIDTH = 3
N_MOD = 6

LANES = 128
SUBLANES = 8
SUBLANES_BF16 = 16
GROUPS_PER_CHUNK = 8
VMEM_LIMIT = 56 * 1024 * 1024
TM_IN = 1024
IN_SPLIT = 4
TQ = 512
ATTN_SUPER = 6
TM_FFN = 512
TT = 64
SSM_PIECE_STEPS = 8
LOG2E = math.log2(math.e)

BF16 = jnp.bfloat16
F32 = jnp.float32


def _split3(a):
    hi = a.astype(BF16)
    r1 = a - hi.astype(F32)
    mid = r1.astype(BF16)
    lo = (r1 - mid.astype(F32)).astype(BF16)
    return hi, mid, lo


def _rms_scale(x):
    return lax.rsqrt(jnp.mean(x * x, axis=-1, keepdims=True) + EPS)


def _mod_kernel(c_ref, w_ref, b_ref, o_ref):
    c = c_ref[...]
    s = c * jax.nn.sigmoid(c)
    w = w_ref[...]
    s_hi = s.astype(BF16)
    s_lo = (s - s_hi.astype(F32)).astype(BF16)
    w_hi = w.astype(BF16)
    w_lo = (w - w_hi.astype(F32)).astype(BF16)
    acc = jnp.dot(s_hi, w_hi, preferred_element_type=F32)
    acc += jnp.dot(s_lo, w_hi, preferred_element_type=F32)
    acc += jnp.dot(s_hi, w_lo, preferred_element_type=F32)
    o_ref[...] = acc + b_ref[...]


def _modulation(c, w_ada, b_ada):
    bsz, d = c.shape
    n = w_ada.shape[1]
    tn = 512
    return pl.pallas_call(
        _mod_kernel,
        out_shape=jax.ShapeDtypeStruct((bsz, n), F32),
        grid=(n // tn,),
        in_specs=[pl.BlockSpec((bsz, d), lambda j: (0, 0)),
                  pl.BlockSpec((d, tn), lambda j: (0, j)),
                  pl.BlockSpec((1, tn), lambda j: (0, j))],
        out_specs=pl.BlockSpec((bsz, tn), lambda j: (0, j)),
        compiler_params=pltpu.CompilerParams(dimension_semantics=("arbitrary",)),
        name="modulation",
    )(c, w_ada, b_ada.reshape(1, n))


def _inproj_kernel(x_ref, mod_ref, g_ref, w_ref, wft_ref, bf_ref, tri_ref,
                   q_ref, k_ref, v_ref, u_ref, carry_ref, *, d, aw):
    si = pl.program_id(1)
    tm = x_ref.shape[1]

    @pl.when(si == 0)
    def _():
        carry_ref[...] = jnp.zeros_like(carry_ref)

    shift = mod_ref[0, :, 0:d]
    scale = mod_ref[0, :, d:2 * d]
    nblk = IN_SPLIT
    th = tm // nblk

    def normed(r):
        x = x_ref[0, r * th:(r + 1) * th, :]
        h = x * _rms_scale(x) * g_ref[...]
        return (h * (1.0 + scale) + shift).astype(BF16)

    one = jnp.ones((1, th), F32)
    sub = lax.broadcasted_iota(jnp.int32, (SUBLANES, th), 0)
    lane = lax.broadcasted_iota(jnp.int32, (th, LANES), 1)
    left = lane < HEAD_DIM
    gap = jnp.zeros((HEAD_DIM - SUBLANES, th), F32)
    ones_first = jnp.where(lane == HEAD_DIM, 1.0, 0.0)
    ones_second = jnp.where(lane == 0, 1.0, 0.0)
    qscale = HEAD_DIM ** -0.5 * LOG2E
    step = 2 * LANES
    sw = u_ref.shape[2]
    nh = bf_ref.shape[0]

    def tile(vals):
        t = jnp.zeros((SUBLANES, th), F32)
        for i, val in enumerate(vals):
            t = jnp.where(sub == i, val, t)
        return t

    def gate_logits(hb):
        z = lax.dot_general(wft_ref[...], hb, (((1,), (1,)), ((), ())),
                            preferred_element_type=F32) + bf_ref[...]
        logf = jnp.minimum(z, 0.0) - jnp.log1p(jnp.exp(-jnp.abs(z)))
        return jnp.concatenate(_split3(logf), axis=0)

    def prefix(pieces, carry):
        cs = jnp.dot(pieces, tri_ref[...], preferred_element_type=F32)
        return cs[0:nh] + cs[nh:2 * nh] + cs[2 * nh:3 * nh] + carry

    hb_next = normed(0)
    cum_next = prefix(gate_logits(hb_next), carry_ref[:, LANES - 1:LANES])
    for r in range(nblk):
        hb, cum = hb_next, cum_next
        rows = slice(r * th, (r + 1) * th)
        if r + 1 < nblk:
            hb_next = normed(r + 1)
            pieces_next = gate_logits(hb_next)
        hi, mid, lo = [p.astype(F32) for p in _split3(cum * LOG2E)]

        def q_tile(hd, hi=hi, mid=mid, lo=lo):
            return tile([hi[hd:hd + 1], mid[hd:hd + 1], lo[hd:hd + 1], one, one, one])

        def k_tile(hd, hi=hi, mid=mid, lo=lo):
            return tile([one, one, one, -hi[hd:hd + 1], -mid[hd:hd + 1], -lo[hd:hd + 1]])

        def piece(c0, width, hb=hb):
            return jnp.dot(hb, w_ref[:, c0:c0 + width], preferred_element_type=F32)

        for c0 in range(0, aw, step):
            pq2 = piece(c0, step) * qscale
            pk2 = piece(aw + c0, step)
            pv2 = piece(2 * aw + c0, step)
            for half in range(2):
                pair = c0 // LANES + half
                first, second = 2 * pair, 2 * pair + 1
                cols = slice(LANES * half, LANES * (half + 1))
                pq, pk, pv = pq2[:, cols], pk2[:, cols], pv2[:, cols]
                qe = jnp.concatenate([q_tile(second), gap, q_tile(first), gap], axis=0).T
                ke = jnp.concatenate([k_tile(second), gap, k_tile(first), gap], axis=0).T
                q_ref[0, first, rows, :] = jnp.where(left, pq, qe).astype(BF16)
                q_ref[0, second, rows, :] = jnp.where(left, qe, pq).astype(BF16)
                k_ref[0, first, rows, :] = jnp.where(left, pk, ke).astype(BF16)
                k_ref[0, second, rows, :] = jnp.where(left, ke, pk).astype(BF16)
                v_ref[0, first, rows, :] = jnp.where(left, pv, ones_first).astype(BF16)
                v_ref[0, second, rows, :] = jnp.where(left, ones_second, pv).astype(BF16)
        for c0 in range(0, sw, step):
            u_ref[0, rows, c0:c0 + step] = piece(3 * aw + c0, step).astype(BF16)
        if r + 1 < nblk:
            cum_next = prefix(pieces_next, cum[:, th - 1:th])
    carry_ref[...] = jnp.broadcast_to(cum[:, th - 1:th], carry_ref.shape)


def _in_proj(x, mod3, g_mix, w_cat, wf_t, b_f, tm):
    bsz, seq, d = x.shape
    ncat = w_cat.shape[1]
    aw = N_HEADS * HEAD_DIM
    sw = ncat - 3 * aw
    nh = wf_t.shape[0]
    tb = tm // IN_SPLIT
    tri = (jnp.arange(tb)[:, None] <= jnp.arange(tb)[None, :]).astype(BF16)
    const = lambda b, s: (0, 0)
    rows = lambda b, s: (b, s, 0)
    heads = lambda b, s: (b, 0, s, 0)
    kern = functools.partial(_inproj_kernel, d=d, aw=aw)
    return pl.pallas_call(
        kern,
        out_shape=(jax.ShapeDtypeStruct((bsz, nh, seq, LANES), BF16),
                   jax.ShapeDtypeStruct((bsz, nh, seq, LANES), BF16),
                   jax.ShapeDtypeStruct((bsz, nh, seq, LANES), BF16),
                   jax.ShapeDtypeStruct((bsz, seq, sw), BF16)),
        grid=(bsz, seq // tm),
        in_specs=[pl.BlockSpec((1, tm, d), rows),
                  pl.BlockSpec((1, 1, mod3.shape[2]), lambda b, s: (b, 0, 0)),
                  pl.BlockSpec((1, d), const),
                  pl.BlockSpec((d, ncat), const, pipeline_mode=pl.Buffered(1)),
                  pl.BlockSpec((nh, d), const),
                  pl.BlockSpec((nh, 1), const),
                  pl.BlockSpec((tb, tb), const, pipeline_mode=pl.Buffered(1))],
        out_specs=(pl.BlockSpec((1, nh, tm, LANES), heads),
                   pl.BlockSpec((1, nh, tm, LANES), heads),
                   pl.BlockSpec((1, nh, tm, LANES), heads),
                   pl.BlockSpec((1, tm, sw), rows)),
        scratch_shapes=[pltpu.VMEM((nh, LANES), F32)],
        compiler_params=pltpu.CompilerParams(
            dimension_semantics=("arbitrary", "arbitrary"), vmem_limit_bytes=VMEM_LIMIT),
        name="in_proj",
    )(x, mod3, g_mix, w_cat, wf_t, b_f, tri)


def _attn_kernel(qx_ref, kx_ref, vx_ref, o_ref, m_ref, acc_ref, s_ref, *, tq):
    seq = kx_ref.shape[2]
    extra_lo = (HEAD_DIM, 0)

    row = lax.broadcasted_iota(jnp.int32, (tq // 2, tq // 2), 0)
    col = lax.broadcasted_iota(jnp.int32, (tq // 2, tq // 2), 1)
    causal = col <= row
    lane = lax.broadcasted_iota(jnp.int32, (tq, LANES), 1)

    hq = tq // 2

    def qk(q0, slot, j0, width, last):
        parts = ((0, hq, width - hq), (hq, hq, width)) if last else ((0, tq, width),)
        for hh in range(2):
            for r0, nr, wd in parts:
                qx = qx_ref[0, hh, q0 + r0:q0 + r0 + nr, :]
                kb = kx_ref[0, hh, j0:j0 + wd, :]
                s_ref[slot, hh, r0:r0 + nr, 0:wd] = lax.dot_general(
                    qx, kb, (((1,), (1,)), ((), ())), preferred_element_type=F32)

    def softmax_pv(par, slot, j0, width, last):
        parts = ((0, hq, width - hq), (hq, hq, width)) if last else ((0, tq, width),)
        for hh in range(2):
            for r0, nr, wd in parts:
                s = s_ref[slot, hh, r0:r0 + nr, 0:wd]
                if last:
                    tail = jnp.where(causal, s[:, wd - hq:], NEG_INF)
                    s = tail if wd == hq else jnp.concatenate([s[:, 0:wd - hq], tail], axis=1)
                m_old = m_ref[par, hh, r0:r0 + nr, :]
                m_new = jnp.maximum(m_old, jnp.max(s, axis=-1, keepdims=True))
                alpha = jnp.exp2(m_old - m_new)
                p = jnp.exp2(s - jnp.concatenate([m_new] * (wd // LANES), axis=1)).astype(BF16)
                vb = vx_ref[0, hh, j0:j0 + wd, :]
                acc_ref[par, hh, r0:r0 + nr, :] = (alpha * acc_ref[par, hh, r0:r0 + nr, :]
                                                   + jnp.dot(p, vb, preferred_element_type=F32))
                m_ref[par, hh, r0:r0 + nr, :] = m_new

    nq = seq // tq
    steps = []
    for qi in range(nq):
        nkeys = (qi + 1) * tq
        j0 = 0
        while j0 < nkeys:
            width = min(ATTN_SUPER * tq, nkeys - j0)
            steps.append((qi, j0, width, j0 + width == nkeys))
            j0 += width
    qk(0, 0, steps[0][1], steps[0][2], steps[0][3])
    for i, (qi, j0, width, last) in enumerate(steps):
        slot = i % 2
        par = qi % 2
        q0 = qi * tq
        if j0 == 0:
            m_ref[par] = jnp.full(m_ref.shape[1:], NEG_INF, F32)
            acc_ref[par] = jnp.zeros(acc_ref.shape[1:], F32)
        if i + 1 < len(steps):
            nqi, nj0, nwidth, nlast = steps[i + 1]
            qk(nqi * tq, 1 - slot, nj0, nwidth, nlast)
        softmax_pv(par, slot, j0, width, last)
        if last:
            a0 = acc_ref[par, 0]
            a1 = acc_ref[par, 1]
            o0 = a0 / a0[:, extra_lo[0]:extra_lo[0] + 1]
            o1 = a1 / a1[:, extra_lo[1]:extra_lo[1] + 1]
            o_ref[0, q0:q0 + tq, :] = jnp.where(lane < HEAD_DIM, o0, o1).astype(BF16)


def _attention(qx, kx, vx, tq):
    bsz, nh, seq, _ = qx.shape
    npair = nh // 2
    kern = functools.partial(_attn_kernel, tq=tq)
    pair = lambda b, h: (b, h, 0, 0)
    return pl.pallas_call(
        kern,
        out_shape=jax.ShapeDtypeStruct((bsz, seq, nh * HEAD_DIM), BF16),
        grid=(bsz, npair),
        in_specs=[pl.BlockSpec((1, 2, seq, LANES), pair),
                  pl.BlockSpec((1, 2, seq, LANES), pair),
                  pl.BlockSpec((1, 2, seq, LANES), pair)],
        out_specs=pl.BlockSpec((1, seq, LANES), lambda b, h: (b, 0, h)),
        scratch_shapes=[pltpu.VMEM((2, 2, tq, LANES), F32),
                        pltpu.VMEM((2, 2, tq, LANES), F32),
                        pltpu.VMEM((2, 2, tq, ATTN_SUPER * tq), F32)],
        compiler_params=pltpu.CompilerParams(
            dimension_semantics=("arbitrary", "arbitrary"), vmem_limit_bytes=VMEM_LIMIT),
        name="fox_attention",
    )(qx, kx, vx)


def _ssm_param_kernel(are_ref, aim_ref, ldt_ref, bre_ref, bim_ref,
                      lre_ref, lim_ref, bbre_ref, bbim_ref):
    a_re = are_ref[...]
    a_im = aim_ref[...]
    dt = jnp.exp(ldt_ref[...])
    mag = jnp.exp(a_re * dt)
    ang = a_im * dt
    l_re = mag * jnp.cos(ang)
    l_im = mag * jnp.sin(ang)
    lre_ref[...] = l_re
    lim_ref[...] = l_im
    n_re = l_re - 1.0
    den = a_re * a_re + a_im * a_im
    c_re = (n_re * a_re + l_im * a_im) / den
    c_im = (l_im * a_re - n_re * a_im) / den
    c_re = c_re[:, None, :]
    c_im = c_im[:, None, :]
    b_re = bre_ref[...]
    b_im = bim_ref[...]
    bbre_ref[...] = c_re * b_re - c_im * b_im
    bbim_ref[...] = c_re * b_im + c_im * b_re


def _ssm_params(a_re, a_im, log_dt, b_re_t, b_im_t):
    g, p = a_re.shape
    c = b_re_t.shape[1]
    return pl.pallas_call(
        _ssm_param_kernel,
        out_shape=(jax.ShapeDtypeStruct((g, p), F32), jax.ShapeDtypeStruct((g, p), F32),
                   jax.ShapeDtypeStruct((g, c, p), F32), jax.ShapeDtypeStruct((g, c, p), F32)),
        name="ssm_params",
    )(a_re, a_im, log_dt.reshape(g, 1), b_re_t, b_im_t)


def _ssm_kernel(u_ref, perm_ref, permt_ref, lre_ref, lim_ref, bmat_ref, cre_ref, cim_ref, dskip_ref,
                wglu_ref, bglu_ref, gout_ref, o_ref, bu_ref, h_ref, sre_ref, sim_ref, *, nb, tt, th):
    nchunk = bmat_ref.shape[0]
    cw = bmat_ref.shape[2] // 2
    iw = bmat_ref.shape[1]
    w = u_ref.shape[2]

    @pl.when(pl.program_id(0) == 0)
    def _():
        sre_ref[...] = jnp.zeros_like(sre_ref)
        sim_ref[...] = jnp.zeros_like(sim_ref)

    parts = []
    for g in range(tt // th):
        ug = u_ref[:, g * th:(g + 1) * th, :].reshape(nb * th, w)
        parts.append(jnp.dot(perm_ref[...], ug, preferred_element_type=F32).astype(BF16))
    u = jnp.concatenate(parts, axis=0)

    spb = SSM_PIECE_STEPS
    rb = spb * nb
    npiece = tt // spb

    def bu(c, k):
        base = 2 * cw * c
        bu_ref[k * rb:(k + 1) * rb, base:base + 2 * cw] = jnp.dot(
            u[k * rb:(k + 1) * rb, iw * c:iw * (c + 1)], bmat_ref[c], preferred_element_type=F32)

    def readout(c, k):
        base = 2 * cw * c
        rows = slice(k * rb, (k + 1) * rb)
        y = jnp.dot(h_ref[rows, base:base + cw], cre_ref[c], preferred_element_type=F32)
        return y - jnp.dot(h_ref[rows, base + cw:base + 2 * cw], cim_ref[c], preferred_element_type=F32)

    ys = [[None] * npiece for _ in range(nchunk)]
    for k in range(npiece):
        bu(0, k)
    for c in range(nchunk + 1):
        if c < nchunk:
            base = 2 * cw * c
            a_re = jnp.broadcast_to(lre_ref[c:c + 1, :], (nb, cw))
            a_im = jnp.broadcast_to(lim_ref[c:c + 1, :], (nb, cw))
            h_re = sre_ref[:, cw * c:cw * (c + 1)]
            h_im = sim_ref[:, cw * c:cw * (c + 1)]
        for k in range(npiece):
            if c + 1 < nchunk:
                bu(c + 1, k)
            if c >= 1:
                ys[c - 1][k] = readout(c - 1, k)
            if c < nchunk:
                for t in range(k * spb, (k + 1) * spb):
                    r0 = t * nb
                    b_re = bu_ref[r0:r0 + nb, base:base + cw]
                    b_im = bu_ref[r0:r0 + nb, base + cw:base + 2 * cw]
                    h_re, h_im = (a_re * h_re - a_im * h_im + b_re, a_re * h_im + a_im * h_re + b_im)
                    h_ref[r0:r0 + nb, base:base + cw] = h_re.astype(BF16)
                    h_ref[r0:r0 + nb, base + cw:base + 2 * cw] = h_im.astype(BF16)
        if c < nchunk:
            sre_ref[:, cw * c:cw * (c + 1)] = h_re
            sim_ref[:, cw * c:cw * (c + 1)] = h_im
    ys = [jnp.concatenate(pieces, axis=0) for pieces in ys]
    y = jnp.concatenate(ys, axis=-1) + dskip_ref[...] * u.astype(F32)
    z = jax.nn.gelu(y, approximate=True)
    gate = jnp.dot(z.astype(BF16), wglu_ref[...], preferred_element_type=F32) + bglu_ref[...]
    out = z * jax.nn.sigmoid(gate)
    out = (out * _rms_scale(out) * gout_ref[...]).astype(BF16)
    for g in range(tt // th):
        og = jnp.dot(permt_ref[...], out[g * th * nb:(g + 1) * th * nb], preferred_element_type=F32)
        o_ref[:, g * th:(g + 1) * th, :] = og.astype(BF16).reshape(nb, th, w)


def _ssm(u, l_re, l_im, bmat, cre, cim, dskip, wglu_bd, bglu, g_out, tt):
    nb, seq, w = u.shape
    th = SUBLANES_BF16
    rows = nb * tt
    nchunk, iw, cw2 = bmat.shape
    cw = cw2 // 2
    src = (jnp.arange(nb)[None, :] * th + jnp.arange(th)[:, None]).reshape(-1)
    perm = (src[:, None] == jnp.arange(nb * th)[None, :]).astype(BF16)
    const2 = lambda i: (0, 0)
    const3 = lambda i: (0, 0, 0)
    kern = functools.partial(_ssm_kernel, nb=nb, tt=tt, th=th)
    return pl.pallas_call(
        kern,
        out_shape=jax.ShapeDtypeStruct((nb, seq, w), BF16),
        grid=(seq // tt,),
        in_specs=[pl.BlockSpec((nb, tt, w), lambda i: (0, i, 0)),
                  pl.BlockSpec((nb * th, nb * th), const2),
                  pl.BlockSpec((nb * th, nb * th), const2),
                  pl.BlockSpec((nchunk, cw), const2),
                  pl.BlockSpec((nchunk, cw), const2),
                  pl.BlockSpec((nchunk, iw, cw2), const3),
                  pl.BlockSpec((nchunk, cw, iw), const3),
                  pl.BlockSpec((nchunk, cw, iw), const3),
                  pl.BlockSpec((1, w), const2),
                  pl.BlockSpec((w, w), const2),
                  pl.BlockSpec((1, w), const2),
                  pl.BlockSpec((1, w), const2)],
        out_specs=pl.BlockSpec((nb, tt, w), lambda i: (0, i, 0)),
        scratch_shapes=[pltpu.VMEM((rows, nchunk * cw2), F32),
                        pltpu.VMEM((rows, nchunk * cw2), BF16),
                        pltpu.VMEM((nb, nchunk * cw), F32),
                        pltpu.VMEM((nb, nchunk * cw), F32)],
        compiler_params=pltpu.CompilerParams(
            dimension_semantics=("arbitrary",), vmem_limit_bytes=VMEM_LIMIT),
        name="s5_ssm",
    )(u, perm, perm.T, l_re, l_im, bmat, cre, cim, dskip, wglu_bd, bglu, g_out)


def _outffn_kernel(x_ref, attn_ref, ssm_ref, mod_ref, gattn_ref, wout_ref, gffn_ref, wup_ref,
                   convw_ref, convb_ref, wdown_ref, gfin_ref, o_ref, carry_ref, *, d, dff):
    si = pl.program_id(1)
    tm = x_ref.shape[1]

    @pl.when(si == 0)
    def _():
        carry_ref[...] = jnp.zeros_like(carry_ref)

    gt_m = mod_ref[0, :, 2 * d:3 * d]
    sh_f = mod_ref[0, :, 3 * d:4 * d]
    sc_f = mod_ref[0, :, 4 * d:5 * d]
    gt_f = mod_ref[0, :, 5 * d:6 * d]

    a = attn_ref[0].astype(F32)
    an = (a * _rms_scale(a) * gattn_ref[...]).astype(BF16)
    mixin = jnp.concatenate([an, ssm_ref[0]], axis=-1)
    mix = jnp.dot(mixin, wout_ref[...], preferred_element_type=F32)
    x1 = x_ref[0] + gt_m * mix

    h = x1 * _rms_scale(x1) * gffn_ref[...]
    h = (h * (1.0 + sc_f) + sh_f).astype(BF16)
    up = jnp.dot(h, wup_ref[...], preferred_element_type=F32)
    gp = up[:, 0:dff]
    val = up[:, dff:2 * dff]

    prev = carry_ref[...]
    r8 = lax.broadcasted_iota(jnp.int32, prev.shape, 0)

    def shifted(k):
        body = pltpu.roll(gp, k, 0)
        head = jnp.where(r8 < k, pltpu.roll(prev, k, 0), body[0:8])
        return jnp.concatenate([head, body[8:]], axis=0)

    conv = (convw_ref[0:1, :] * shifted(2) + convw_ref[1:2, :] * shifted(1)
            + convw_ref[2:3, :] * gp + convb_ref[...])
    carry_ref[...] = gp[tm - 8:tm]
    act = (conv * jax.nn.sigmoid(conv) * val).astype(BF16)
    y = jnp.dot(act, wdown_ref[...], preferred_element_type=F32)
    x2 = x1 + gt_f * y
    o_ref[0] = x2 * _rms_scale(x2) * gfin_ref[...]


def _out_ffn(x, attn, ssm, mod3, g_attn, w_out, g_ffn, w_up, conv_w, conv_b, w_down, g_final, tm):
    bsz, seq, d = x.shape
    aw = attn.shape[2]
    sw = ssm.shape[2]
    dff = w_down.shape[0]
    const = lambda b, s: (0, 0)
    rows = lambda b, s: (b, s, 0)
    single = pl.Buffered(1)
    kern = functools.partial(_outffn_kernel, d=d, dff=dff)
    return pl.pallas_call(
        kern,
        out_shape=jax.ShapeDtypeStruct((bsz, seq, d), F32),
        grid=(bsz, seq // tm),
        in_specs=[pl.BlockSpec((1, tm, d), rows),
                  pl.BlockSpec((1, tm, aw), rows),
                  pl.BlockSpec((1, tm, sw), rows),
                  pl.BlockSpec((1, 1, mod3.shape[2]), lambda b, s: (b, 0, 0)),
                  pl.BlockSpec((1, aw), const),
                  pl.BlockSpec((aw + sw, d), const, pipeline_mode=single),
                  pl.BlockSpec((1, d), const),
                  pl.BlockSpec((d, 2 * dff), const, pipeline_mode=single),
                  pl.BlockSpec((CONV_WIDTH, dff), const),
                  pl.BlockSpec((1, dff), const),
                  pl.BlockSpec((dff, d), const, pipeline_mode=single),
                  pl.BlockSpec((1, d), const)],
        out_specs=pl.BlockSpec((1, tm, d), rows),
        scratch_shapes=[pltpu.VMEM((8, dff), F32)],
        compiler_params=pltpu.CompilerParams(
            dimension_semantics=("arbitrary", "arbitrary"), vmem_limit_bytes=VMEM_LIMIT),
        name="out_ffn",
    )(x, attn, ssm, mod3, g_attn, w_out, g_ffn, w_up, conv_w, conv_b, w_down, g_final)


def _block_diag(blocks):
    n, r, c = blocks.shape
    eye = jnp.eye(n, dtype=blocks.dtype)
    return (blocks[:, :, None, :] * eye[:, None, :, None]).reshape(n * r, n * c)


def _chunked_block_diag(blocks, per):
    g, r, c = blocks.shape
    return jax.vmap(_block_diag)(blocks.reshape(g // per, per, r, c))


def kernel(x, c, w_ada, b_ada, g_mix, w_in, b_fgate, a_re, a_im, log_dt, ssm_b_re, ssm_b_im,
           ssm_c_re, ssm_c_im, d_skip, w_glu, b_glu, g_attn_out, g_ssm_out, w_out, g_ffn, w_up,
           conv_w, conv_b, w_down, g_final):
    bsz, seq, d = x.shape
    depth = w_ada.shape[0]
    aw = N_HEADS * HEAD_DIM
    ngroups = a_re.shape[1]
    sw = ngroups * SSM_GROUP
    per = GROUPS_PER_CHUNK
    tm_in = min(TM_IN, seq)
    tq = min(TQ, seq)
    tm_ffn = min(TM_FFN, seq)
    tt = min(TT, seq)
    assert bsz % 8 == 0 and seq % tm_in == 0 and seq % tq == 0 and ngroups % per == 0
    assert tq % (2 * LANES) == 0
    assert seq % tt == 0 and tt % SUBLANES_BF16 == 0

    for l in range(depth):
        mod3 = _modulation(c, w_ada[l], b_ada[l]).reshape(bsz, 1, N_MOD * d)

        w = w_in[l]
        w_cat = jnp.concatenate([w[:, 0:3 * aw], w[:, 3 * aw + N_HEADS:]], axis=1).astype(BF16)
        wf_t = w[:, 3 * aw:3 * aw + N_HEADS].T.astype(BF16)
        qx, kx, vx, u = _in_proj(x, mod3, g_mix[l].reshape(1, d), w_cat, wf_t,
                                 b_fgate[l].reshape(N_HEADS, 1), tm_in)

        attn = _attention(qx, kx, vx, tq)

        l_re, l_im, bb_re, bb_im = _ssm_params(
            a_re[l], a_im[l], log_dt[l],
            ssm_b_re[l].transpose(0, 2, 1), ssm_b_im[l].transpose(0, 2, 1))
        bmat = jnp.concatenate([_chunked_block_diag(bb_re, per), _chunked_block_diag(bb_im, per)],
                               axis=-1).astype(BF16)
        cre = _chunked_block_diag(ssm_c_re[l].transpose(0, 2, 1), per).astype(BF16)
        cim = _chunked_block_diag(ssm_c_im[l].transpose(0, 2, 1), per).astype(BF16)
        wglu_bd = _block_diag(w_glu[l]).astype(BF16)
        ssm = _ssm(u, l_re.reshape(ngroups // per, per * STATE_DIM),
                   l_im.reshape(ngroups // per, per * STATE_DIM),
                   bmat, cre, cim, d_skip[l].reshape(1, sw), wglu_bd, b_glu[l].reshape(1, sw),
                   g_ssm_out[l].reshape(1, sw), tt)

        assert depth == 1
        x = _out_ffn(x, attn, ssm, mod3, g_attn_out[l].reshape(1, aw), w_out[l].astype(BF16),
                     g_ffn[l].reshape(1, d), w_up[l].astype(BF16), conv_w[l], conv_b[l].reshape(1, -1),
                     w_down[l].astype(BF16), g_final.reshape(1, d), tm_ffn)
    return x
```

```python
import functools
import math

import jax
import jax.numpy as jnp
from jax import lax
from jax.experimental import pallas as pl
from jax.experimental.pallas import tpu as pltpu

EPS = 1e-6
NEG_INF = -1e30
HEAD_DIM = 64
N_HEADS = 8
SSM_GROUP = 16
STATE_DIM = 64
CONV_WIDTH = 3
N_MOD = 6

LANES = 128
SUBLANES = 8
SUBLANES_BF16 = 16
GROUPS_PER_CHUNK = 8
VMEM_LIMIT = 56 * 1024 * 1024
TM_IN = 1024
IN_SPLIT = 4
TQ = 512
ATTN_SUPER = 4
TM_FFN = 512
FFN_SPLIT = 2
TT = 64
SSM_PIECE_STEPS = 8
LOG2E = math.log2(math.e)

BF16 = jnp.bfloat16
F32 = jnp.float32


def _split3(a):
    hi = a.astype(BF16)
    r1 = a - hi.astype(F32)
    mid = r1.astype(BF16)
    lo = (r1 - mid.astype(F32)).astype(BF16)
    return hi, mid, lo


def _rms_scale(x):
    return lax.rsqrt(jnp.mean(x * x, axis=-1, keepdims=True) + EPS)


def _mod_kernel(c_ref, w_ref, b_ref, o_ref):
    c = c_ref[...]
    s = c * jax.nn.sigmoid(c)
    w = w_ref[...]
    s_hi = s.astype(BF16)
    s_lo = (s - s_hi.astype(F32)).astype(BF16)
    w_hi = w.astype(BF16)
    w_lo = (w - w_hi.astype(F32)).astype(BF16)
    acc = jnp.dot(s_hi, w_hi, preferred_element_type=F32)
    acc += jnp.dot(s_lo, w_hi, preferred_element_type=F32)
    acc += jnp.dot(s_hi, w_lo, preferred_element_type=F32)
    o_ref[...] = acc + b_ref[...]


def _modulation(c, w_ada, b_ada):
    bsz, d = c.shape
    n = w_ada.shape[1]
    tn = 512
    return pl.pallas_call(
        _mod_kernel,
        out_shape=jax.ShapeDtypeStruct((bsz, n), F32),
        grid=(n // tn,),
        in_specs=[pl.BlockSpec((bsz, d), lambda j: (0, 0)),
                  pl.BlockSpec((d, tn), lambda j: (0, j)),
                  pl.BlockSpec((1, tn), lambda j: (0, j))],
        out_specs=pl.BlockSpec((bsz, tn), lambda j: (0, j)),
        compiler_params=pltpu.CompilerParams(dimension_semantics=("arbitrary",)),
        name="modulation",
    )(c, w_ada, b_ada.reshape(1, n))


def _inproj_kernel(x_ref, mod_ref, g_ref, w_ref, wft_ref, bf_ref, tri_ref,
                   q_ref, k_ref, v_ref, u_ref, carry_ref, *, d, aw):
    si = pl.program_id(1)
    tm = x_ref.shape[1]

    @pl.when(si == 0)
    def _():
        carry_ref[...] = jnp.zeros_like(carry_ref)

    shift = mod_ref[0, :, 0:d]
    scale = mod_ref[0, :, d:2 * d]
    nblk = IN_SPLIT
    th = tm // nblk

    def normed(r):
        x = x_ref[0, r * th:(r + 1) * th, :]
        h = x * _rms_scale(x) * g_ref[...]
        return (h * (1.0 + scale) + shift).astype(BF16)

    one = jnp.ones((1, th), F32)
    sub = lax.broadcasted_iota(jnp.int32, (SUBLANES, th), 0)
    lane = lax.broadcasted_iota(jnp.int32, (th, LANES), 1)
    left = lane < HEAD_DIM
    gap = jnp.zeros((HEAD_DIM - SUBLANES, th), F32)
    ones_first = jnp.where(lane == HEAD_DIM, 1.0, 0.0)
    ones_second = jnp.where(lane == 0, 1.0, 0.0)
    qscale = HEAD_DIM ** -0.5 * LOG2E
    step = 2 * LANES
    sw = u_ref.shape[2]
    nh = bf_ref.shape[0]

    def tile(vals):
        t = jnp.zeros((SUBLANES, th), F32)
        for i, val in enumerate(vals):
            t = jnp.where(sub == i, val, t)
        return t

    def gate_logits(hb):
        z = lax.dot_general(wft_ref[...], hb, (((1,), (1,)), ((), ())),
                            preferred_element_type=F32) + bf_ref[...]
        logf = jnp.minimum(z, 0.0) - jnp.log1p(jnp.exp(-jnp.abs(z)))
        return jnp.concatenate(_split3(logf), axis=0)

    def prefix(pieces, carry):
        cs = jnp.dot(pieces, tri_ref[...], preferred_element_type=F32)
        return cs[0:nh] + cs[nh:2 * nh] + cs[2 * nh:3 * nh] + carry

    hb_next = normed(0)
    cum_next = prefix(gate_logits(hb_next), carry_ref[:, LANES - 1:LANES])
    for r in range(nblk):
        hb, cum = hb_next, cum_next
        rows = slice(r * th, (r + 1) * th)
        if r + 1 < nblk:
            hb_next = normed(r + 1)
            pieces_next = gate_logits(hb_next)
        hi, mid, lo = [p.astype(F32) for p in _split3(cum * LOG2E)]

        def q_tile(hd, hi=hi, mid=mid, lo=lo):
            return tile([hi[hd:hd + 1], mid[hd:hd + 1], lo[hd:hd + 1], one, one, one])

        def k_tile(hd, hi=hi, mid=mid, lo=lo):
            return tile([one, one, one, -hi[hd:hd + 1], -mid[hd:hd + 1], -lo[hd:hd + 1]])

        def piece(c0, width, hb=hb):
            return jnp.dot(hb, w_ref[:, c0:c0 + width], preferred_element_type=F32)

        for c0 in range(0, aw, step):
            pq2 = piece(c0, step) * qscale
            pk2 = piece(aw + c0, step)
            pv2 = piece(2 * aw + c0, step)
            for half in range(2):
                pair = c0 // LANES + half
                first, second = 2 * pair, 2 * pair + 1
                cols = slice(LANES * half, LANES * (half + 1))
                pq, pk, pv = pq2[:, cols], pk2[:, cols], pv2[:, cols]
                qe = jnp.concatenate([q_tile(second), gap, q_tile(first), gap], axis=0).T
                ke = jnp.concatenate([k_tile(second), gap, k_tile(first), gap], axis=0).T
                q_ref[0, first, rows, :] = jnp.where(left, pq, qe).astype(BF16)
                q_ref[0, second, rows, :] = jnp.where(left, qe, pq).astype(BF16)
                k_ref[0, first, rows, :] = jnp.where(left, pk, ke).astype(BF16)
                k_ref[0, second, rows, :] = jnp.where(left, ke, pk).astype(BF16)
                v_ref[0, first, rows, :] = jnp.where(left, pv, ones_first).astype(BF16)
                v_ref[0, second, rows, :] = jnp.where(left, ones_second, pv).astype(BF16)
        for c0 in range(0, sw, step):
            u_ref[0, rows, c0:c0 + step] = piece(3 * aw + c0, step).astype(BF16)
        if r + 1 < nblk:
            cum_next = prefix(pieces_next, cum[:, th - 1:th])
    carry_ref[...] = jnp.broadcast_to(cum[:, th - 1:th], carry_ref.shape)


def _in_proj(x, mod3, g_mix, w_cat, wf_t, b_f, tm):
    bsz, seq, d = x.shape
    ncat = w_cat.shape[1]
    aw = N_HEADS * HEAD_DIM
    sw = ncat - 3 * aw
    nh = wf_t.shape[0]
    tb = tm // IN_SPLIT
    tri = (jnp.arange(tb)[:, None] <= jnp.arange(tb)[None, :]).astype(BF16)
    const = lambda b, s: (0, 0)
    rows = lambda b, s: (b, s, 0)
    heads = lambda b, s: (b, 0, s, 0)
    kern = functools.partial(_inproj_kernel, d=d, aw=aw)
    return pl.pallas_call(
        kern,
        out_shape=(jax.ShapeDtypeStruct((bsz, nh, seq, LANES), BF16),
                   jax.ShapeDtypeStruct((bsz, nh, seq, LANES), BF16),
                   jax.ShapeDtypeStruct((bsz, nh, seq, LANES), BF16),
                   jax.ShapeDtypeStruct((bsz, seq, sw), BF16)),
        grid=(bsz, seq // tm),
        in_specs=[pl.BlockSpec((1, tm, d), rows),
                  pl.BlockSpec((1, 1, mod3.shape[2]), lambda b, s: (b, 0, 0)),
                  pl.BlockSpec((1, d), const),
                  pl.BlockSpec((d, ncat), const, pipeline_mode=pl.Buffered(1)),
                  pl.BlockSpec((nh, d), const),
                  pl.BlockSpec((nh, 1), const),
                  pl.BlockSpec((tb, tb), const, pipeline_mode=pl.Buffered(1))],
        out_specs=(pl.BlockSpec((1, nh, tm, LANES), heads),
                   pl.BlockSpec((1, nh, tm, LANES), heads),
                   pl.BlockSpec((1, nh, tm, LANES), heads),
                   pl.BlockSpec((1, tm, sw), rows)),
        scratch_shapes=[pltpu.VMEM((nh, LANES), F32)],
        compiler_params=pltpu.CompilerParams(
            dimension_semantics=("arbitrary", "arbitrary"), vmem_limit_bytes=VMEM_LIMIT),
        name="in_proj",
    )(x, mod3, g_mix, w_cat, wf_t, b_f, tri)


def _attn_kernel(qx_ref, kx_ref, vx_ref, o_ref, m_ref, acc_ref, s_ref, *, tq):
    seq = kx_ref.shape[2]
    extra_lo = (HEAD_DIM, 0)

    row = lax.broadcasted_iota(jnp.int32, (tq // 2, tq // 2), 0)
    col = lax.broadcasted_iota(jnp.int32, (tq // 2, tq // 2), 1)
    causal = col <= row
    lane = lax.broadcasted_iota(jnp.int32, (tq, LANES), 1)

    hq = tq // 2

    def qk(q0, slot, j0, width, last):
        parts = ((0, hq, width - hq), (hq, hq, width)) if last else ((0, tq, width),)
        for hh in range(2):
            for r0, nr, wd in parts:
                qx = qx_ref[0, hh, q0 + r0:q0 + r0 + nr, :]
                kb = kx_ref[0, hh, j0:j0 + wd, :]
                s_ref[slot, hh, r0:r0 + nr, 0:wd] = lax.dot_general(
                    qx, kb, (((1,), (1,)), ((), ())), preferred_element_type=F32)

    def softmax_pv(par, slot, j0, width, last):
        parts = ((0, hq, width - hq), (hq, hq, width)) if last else ((0, tq, width),)
        for hh in range(2):
            for r0, nr, wd in parts:
                s = s_ref[slot, hh, r0:r0 + nr, 0:wd]
                if last:
                    tail = jnp.where(causal, s[:, wd - hq:], NEG_INF)
                    s = tail if wd == hq else jnp.concatenate([s[:, 0:wd - hq], tail], axis=1)
                m_old = m_ref[par, hh, r0:r0 + nr, :]
                m_new = jnp.maximum(m_old, jnp.max(s, axis=-1, keepdims=True))
                alpha = jnp.exp2(m_old - m_new)
                p = jnp.exp2(s - jnp.concatenate([m_new] * (wd // LANES), axis=1)).astype(BF16)
                vb = vx_ref[0, hh, j0:j0 + wd, :]
                acc_ref[par, hh, r0:r0 + nr, :] = (alpha * acc_ref[par, hh, r0:r0 + nr, :]
                                                   + jnp.dot(p, vb, preferred_element_type=F32))
                m_ref[par, hh, r0:r0 + nr, :] = m_new

    nq = seq // tq
    steps = []
    for qi in range(nq):
        nkeys = (qi + 1) * tq
        j0 = 0
        while j0 < nkeys:
            width = min(ATTN_SUPER * tq, nkeys - j0)
            steps.append((qi, j0, width, j0 + width == nkeys))
            j0 += width
    qk(0, 0, steps[0][1], steps[0][2], steps[0][3])
    for i, (qi, j0, width, last) in enumerate(steps):
        slot = i % 2
        par = qi % 2
        q0 = qi * tq
        if j0 == 0:
            m_ref[par] = jnp.full(m_ref.shape[1:], NEG_INF, F32)
            acc_ref[par] = jnp.zeros(acc_ref.shape[1:], F32)
        if i + 1 < len(steps):
            nqi, nj0, nwidth, nlast = steps[i + 1]
            qk(nqi * tq, 1 - slot, nj0, nwidth, nlast)
        softmax_pv(par, slot, j0, width, last)
        if last:
            a0 = acc_ref[par, 0]
            a1 = acc_ref[par, 1]
            o0 = a0 / a0[:, extra_lo[0]:extra_lo[0] + 1]
            o1 = a1 / a1[:, extra_lo[1]:extra_lo[1] + 1]
            o_ref[0, q0:q0 + tq, :] = jnp.where(lane < HEAD_DIM, o0, o1).astype(BF16)


def _attention(qx, kx, vx, tq):
    bsz, nh, seq, _ = qx.shape
    npair = nh // 2
    kern = functools.partial(_attn_kernel, tq=tq)
    pair = lambda b, h: (b, h, 0, 0)
    return pl.pallas_call(
        kern,
        out_shape=jax.ShapeDtypeStruct((bsz, seq, nh * HEAD_DIM), BF16),
        grid=(bsz, npair),
        in_specs=[pl.BlockSpec((1, 2, seq, LANES), pair),
                  pl.BlockSpec((1, 2, seq, LANES), pair),
                  pl.BlockSpec((1, 2, seq, LANES), pair)],
        out_specs=pl.BlockSpec((1, seq, LANES), lambda b, h: (b, 0, h)),
        scratch_shapes=[pltpu.VMEM((2, 2, tq, LANES), F32),
                        pltpu.VMEM((2, 2, tq, LANES), F32),
                        pltpu.VMEM((2, 2, tq, ATTN_SUPER * tq), F32)],
        compiler_params=pltpu.CompilerParams(
            dimension_semantics=("arbitrary", "arbitrary"), vmem_limit_bytes=VMEM_LIMIT),
        name="fox_attention",
    )(qx, kx, vx)


def _ssm_param_kernel(are_ref, aim_ref, ldt_ref, bre_ref, bim_ref,
                      lre_ref, lim_ref, bbre_ref, bbim_ref):
    a_re = are_ref[...]
    a_im = aim_ref[...]
    dt = jnp.exp(ldt_ref[...])
    mag = jnp.exp(a_re * dt)
    ang = a_im * dt
    l_re = mag * jnp.cos(ang)
    l_im = mag * jnp.sin(ang)
    lre_ref[...] = l_re
    lim_ref[...] = l_im
    n_re = l_re - 1.0
    den = a_re * a_re + a_im * a_im
    c_re = (n_re * a_re + l_im * a_im) / den
    c_im = (l_im * a_re - n_re * a_im) / den
    c_re = c_re[:, None, :]
    c_im = c_im[:, None, :]
    b_re = bre_ref[...]
    b_im = bim_ref[...]
    bbre_ref[...] = c_re * b_re - c_im * b_im
    bbim_ref[...] = c_re * b_im + c_im * b_re


def _ssm_params(a_re, a_im, log_dt, b_re_t, b_im_t):
    g, p = a_re.shape
    c = b_re_t.shape[1]
    return pl.pallas_call(
        _ssm_param_kernel,
        out_shape=(jax.ShapeDtypeStruct((g, p), F32), jax.ShapeDtypeStruct((g, p), F32),
                   jax.ShapeDtypeStruct((g, c, p), F32), jax.ShapeDtypeStruct((g, c, p), F32)),
        name="ssm_params",
    )(a_re, a_im, log_dt.reshape(g, 1), b_re_t, b_im_t)


def _ssm_kernel(u_ref, perm_ref, permt_ref, lre_ref, lim_ref, bmat_ref, cre_ref, cim_ref, dskip_ref,
                wglu_ref, bglu_ref, gout_ref, o_ref, bu_ref, h_ref, sre_ref, sim_ref, *, nb, tt, th):
    nchunk = bmat_ref.shape[0]
    cw = bmat_ref.shape[2] // 2
    iw = bmat_ref.shape[1]
    w = u_ref.shape[2]

    @pl.when(pl.program_id(0) == 0)
    def _():
        sre_ref[...] = jnp.zeros_like(sre_ref)
        sim_ref[...] = jnp.zeros_like(sim_ref)

    parts = []
    for g in range(tt // th):
        ug = u_ref[:, g * th:(g + 1) * th, :].reshape(nb * th, w)
        parts.append(jnp.dot(perm_ref[...], ug, preferred_element_type=F32).astype(BF16))
    u = jnp.concatenate(parts, axis=0)

    spb = SSM_PIECE_STEPS
    rb = spb * nb
    npiece = tt // spb

    def bu(c, k):
        base = 2 * cw * c
        bu_ref[k * rb:(k + 1) * rb, base:base + 2 * cw] = jnp.dot(
            u[k * rb:(k + 1) * rb, iw * c:iw * (c + 1)], bmat_ref[c], preferred_element_type=F32)

    def readout(c, k):
        base = 2 * cw * c
        rows = slice(k * rb, (k + 1) * rb)
        y = jnp.dot(h_ref[rows, base:base + cw], cre_ref[c], preferred_element_type=F32)
        return y - jnp.dot(h_ref[rows, base + cw:base + 2 * cw], cim_ref[c], preferred_element_type=F32)

    ys = [[None] * npiece for _ in range(nchunk)]
    for k in range(npiece):
        bu(0, k)
    for c in range(nchunk + 1):
        if c < nchunk:
            base = 2 * cw * c
            a_re = jnp.broadcast_to(lre_ref[c:c + 1, :], (nb, cw))
            a_im = jnp.broadcast_to(lim_ref[c:c + 1, :], (nb, cw))
            h_re = sre_ref[:, cw * c:cw * (c + 1)]
            h_im = sim_ref[:, cw * c:cw * (c + 1)]
        for k in range(npiece):
            if c + 1 < nchunk:
                bu(c + 1, k)
            if c >= 1:
                ys[c - 1][k] = readout(c - 1, k)
            if c < nchunk:
                for t in range(k * spb, (k + 1) * spb):
                    r0 = t * nb
                    b_re = bu_ref[r0:r0 + nb, base:base + cw]
                    b_im = bu_ref[r0:r0 + nb, base + cw:base + 2 * cw]
                    h_re, h_im = (a_re * h_re - a_im * h_im + b_re, a_re * h_im + a_im * h_re + b_im)
                    h_ref[r0:r0 + nb, base:base + cw] = h_re.astype(BF16)
                    h_ref[r0:r0 + nb, base + cw:base + 2 * cw] = h_im.astype(BF16)
        if c < nchunk:
            sre_ref[:, cw * c:cw * (c + 1)] = h_re
            sim_ref[:, cw * c:cw * (c + 1)] = h_im
    ys = [jnp.concatenate(pieces, axis=0) for pieces in ys]
    y = jnp.concatenate(ys, axis=-1) + dskip_ref[...] * u.astype(F32)
    z = jax.nn.gelu(y, approximate=True)
    gate = jnp.dot(z.astype(BF16), wglu_ref[...], preferred_element_type=F32) + bglu_ref[...]
    out = z * jax.nn.sigmoid(gate)
    out = (out * _rms_scale(out) * gout_ref[...]).astype(BF16)
    for g in range(tt // th):
        og = jnp.dot(permt_ref[...], out[g * th * nb:(g + 1) * th * nb], preferred_element_type=F32)
        o_ref[:, g * th:(g + 1) * th, :] = og.astype(BF16).reshape(nb, th, w)


def _ssm(u, l_re, l_im, bmat, cre, cim, dskip, wglu_bd, bglu, g_out, tt):
    nb, seq, w = u.shape
    th = SUBLANES_BF16
    rows = nb * tt
    nchunk, iw, cw2 = bmat.shape
    cw = cw2 // 2
    src = (jnp.arange(nb)[None, :] * th + jnp.arange(th)[:, None]).reshape(-1)
    perm = (src[:, None] == jnp.arange(nb * th)[None, :]).astype(BF16)
    const2 = lambda i: (0, 0)
    const3 = lambda i: (0, 0, 0)
    kern = functools.partial(_ssm_kernel, nb=nb, tt=tt, th=th)
    return pl.pallas_call(
        kern,
        out_shape=jax.ShapeDtypeStruct((nb, seq, w), BF16),
        grid=(seq // tt,),
        in_specs=[pl.BlockSpec((nb, tt, w), lambda i: (0, i, 0)),
                  pl.BlockSpec((nb * th, nb * th), const2),
                  pl.BlockSpec((nb * th, nb * th), const2),
                  pl.BlockSpec((nchunk, cw), const2),
                  pl.BlockSpec((nchunk, cw), const2),
                  pl.BlockSpec((nchunk, iw, cw2), const3),
                  pl.BlockSpec((nchunk, cw, iw), const3),
                  pl.BlockSpec((nchunk, cw, iw), const3),
                  pl.BlockSpec((1, w), const2),
                  pl.BlockSpec((w, w), const2),
                  pl.BlockSpec((1, w), const2),
                  pl.BlockSpec((1, w), const2)],
        out_specs=pl.BlockSpec((nb, tt, w), lambda i: (0, i, 0)),
        scratch_shapes=[pltpu.VMEM((rows, nchunk * cw2), F32),
                        pltpu.VMEM((rows, nchunk * cw2), BF16),
                        pltpu.VMEM((nb, nchunk * cw), F32),
                        pltpu.VMEM((nb, nchunk * cw), F32)],
        compiler_params=pltpu.CompilerParams(
            dimension_semantics=("arbitrary",), vmem_limit_bytes=VMEM_LIMIT),
        name="s5_ssm",
    )(u, perm, perm.T, l_re, l_im, bmat, cre, cim, dskip, wglu_bd, bglu, g_out)


def _outffn_kernel(x_ref, attn_ref, ssm_ref, mod_ref, gattn_ref, wout_ref, gffn_ref, wup_ref,
                   convw_ref, convb_ref, wdown_ref, gfin_ref, o_ref, carry_ref, *, d, dff):
    si = pl.program_id(1)
    tm = x_ref.shape[1]

    @pl.when(si == 0)
    def _():
        carry_ref[...] = jnp.zeros_like(carry_ref)

    gt_m = mod_ref[0, :, 2 * d:3 * d]
    sh_f = mod_ref[0, :, 3 * d:4 * d]
    sc_f = mod_ref[0, :, 4 * d:5 * d]
    gt_f = mod_ref[0, :, 5 * d:6 * d]
    th = tm // FFN_SPLIT

    def front(r):
        rows = slice(r * th, (r + 1) * th)
        a = attn_ref[0, rows, :].astype(F32)
        an = (a * _rms_scale(a) * gattn_ref[...]).astype(BF16)
        mixin = jnp.concatenate([an, ssm_ref[0, rows, :]], axis=-1)
        mix = jnp.dot(mixin, wout_ref[...], preferred_element_type=F32)
        x1 = x_ref[0, rows, :] + gt_m * mix
        h = x1 * _rms_scale(x1) * gffn_ref[...]
        return x1, (h * (1.0 + sc_f) + sh_f).astype(BF16)

    prev = carry_ref[...]
    r8 = lax.broadcasted_iota(jnp.int32, prev.shape, 0)
    nxt = front(0)
    for r in range(FFN_SPLIT):
        x1, h = nxt
        if r + 1 < FFN_SPLIT:
            nxt = front(r + 1)
        up = jnp.dot(h, wup_ref[...], preferred_element_type=F32)
        gp = up[:, 0:dff]
        val = up[:, dff:2 * dff]

        def shifted(k, gp=gp, prev=prev):
            body = pltpu.roll(gp, k, 0)
            head = jnp.where(r8 < k, pltpu.roll(prev, k, 0), body[0:8])
            return jnp.concatenate([head, body[8:]], axis=0)

        conv = (convw_ref[0:1, :] * shifted(2) + convw_ref[1:2, :] * shifted(1)
                + convw_ref[2:3, :] * gp + convb_ref[...])
        prev = gp[th - 8:th]
        act = (conv * jax.nn.sigmoid(conv) * val).astype(BF16)
        y = jnp.dot(act, wdown_ref[...], preferred_element_type=F32)
        x2 = x1 + gt_f * y
        o_ref[0, r * th:(r + 1) * th, :] = x2 * _rms_scale(x2) * gfin_ref[...]
    carry_ref[...] = prev


def _out_ffn(x, attn, ssm, mod3, g_attn, w_out, g_ffn, w_up, conv_w, conv_b, w_down, g_final, tm):
    bsz, seq, d = x.shape
    aw = attn.shape[2]
    sw = ssm.shape[2]
    dff = w_down.shape[0]
    const = lambda b, s: (0, 0)
    rows = lambda b, s: (b, s, 0)
    single = pl.Buffered(1)
    kern = functools.partial(_outffn_kernel, d=d, dff=dff)
    return pl.pallas_call(
        kern,
        out_shape=jax.ShapeDtypeStruct((bsz, seq, d), F32),
        grid=(bsz, seq // tm),
        in_specs=[pl.BlockSpec((1, tm, d), rows),
                  pl.BlockSpec((1, tm, aw), rows),
                  pl.BlockSpec((1, tm, sw), rows),
                  pl.BlockSpec((1, 1, mod3.shape[2]), lambda b, s: (b, 0, 0)),
                  pl.BlockSpec((1, aw), const),
                  pl.BlockSpec((aw + sw, d), const, pipeline_mode=single),
                  pl.BlockSpec((1, d), const),
                  pl.BlockSpec((d, 2 * dff), const, pipeline_mode=single),
                  pl.BlockSpec((CONV_WIDTH, dff), const),
                  pl.BlockSpec((1, dff), const),
                  pl.BlockSpec((dff, d), const, pipeline_mode=single),
                  pl.BlockSpec((1, d), const)],
        out_specs=pl.BlockSpec((1, tm, d), rows),
        scratch_shapes=[pltpu.VMEM((8, dff), F32)],
        compiler_params=pltpu.CompilerParams(
            dimension_semantics=("arbitrary", "arbitrary"), vmem_limit_bytes=VMEM_LIMIT),
        name="out_ffn",
    )(x, attn, ssm, mod3, g_attn, w_out, g_ffn, w_up, conv_w, conv_b, w_down, g_final)


def _block_diag(blocks):
    n, r, c = blocks.shape
    eye = jnp.eye(n, dtype=blocks.dtype)
    return (blocks[:, :, None, :] * eye[:, None, :, None]).reshape(n * r, n * c)


def _chunked_block_diag(blocks, per):
    g, r, c = blocks.shape
    return jax.vmap(_block_diag)(blocks.reshape(g // per, per, r, c))


def kernel(x, c, w_ada, b_ada, g_mix, w_in, b_fgate, a_re, a_im, log_dt, ssm_b_re, ssm_b_im,
           ssm_c_re, ssm_c_im, d_skip, w_glu, b_glu, g_attn_out, g_ssm_out, w_out, g_ffn, w_up,
           conv_w, conv_b, w_down, g_final):
    bsz, seq, d = x.shape
    depth = w_ada.shape[0]
    aw = N_HEADS * HEAD_DIM
    ngroups = a_re.shape[1]
    sw = ngroups * SSM_GROUP
    per = GROUPS_PER_CHUNK
    tm_in = min(TM_IN, seq)
    tq = min(TQ, seq)
    tm_ffn = min(TM_FFN, seq)
    tt = min(TT, seq)
    assert bsz % 8 == 0 and seq % tm_in == 0 and seq % tq == 0 and ngroups % per == 0
    assert tq % (2 * LANES) == 0
    assert seq % tt == 0 and tt % SUBLANES_BF16 == 0

    for l in range(depth):
        mod3 = _modulation(c, w_ada[l], b_ada[l]).reshape(bsz, 1, N_MOD * d)

        w = w_in[l]
        w_cat = jnp.concatenate([w[:, 0:3 * aw], w[:, 3 * aw + N_HEADS:]], axis=1).astype(BF16)
        wf_t = w[:, 3 * aw:3 * aw + N_HEADS].T.astype(BF16)
        qx, kx, vx, u = _in_proj(x, mod3, g_mix[l].reshape(1, d), w_cat, wf_t,
                                 b_fgate[l].reshape(N_HEADS, 1), tm_in)

        attn = _attention(qx, kx, vx, tq)

        l_re, l_im, bb_re, bb_im = _ssm_params(
            a_re[l], a_im[l], log_dt[l],
            ssm_b_re[l].transpose(0, 2, 1), ssm_b_im[l].transpose(0, 2, 1))
        bmat = jnp.concatenate([_chunked_block_diag(bb_re, per), _chunked_block_diag(bb_im, per)],
                               axis=-1).astype(BF16)
        cre = _chunked_block_diag(ssm_c_re[l].transpose(0, 2, 1), per).astype(BF16)
        cim = _chunked_block_diag(ssm_c_im[l].transpose(0, 2, 1), per).astype(BF16)
        wglu_bd = _block_diag(w_glu[l]).astype(BF16)
        ssm = _ssm(u, l_re.reshape(ngroups // per, per * STATE_DIM),
                   l_im.reshape(ngroups // per, per * STATE_DIM),
                   bmat, cre, cim, d_skip[l].reshape(1, sw), wglu_bd, b_glu[l].reshape(1, sw),
                   g_ssm_out[l].reshape(1, sw), tt)

        assert depth == 1
        x = _out_ffn(x, attn, ssm, mod3, g_attn_out[l].reshape(1, aw), w_out[l].astype(BF16),
                     g_ffn[l].reshape(1, d), w_up[l].astype(BF16), conv_w[l], conv_b[l].reshape(1, -1),
                     w_down[l].astype(BF16), g_final.reshape(1, d), tm_ffn)
    return x
```
